```python
import jax
import jax.numpy as jnp
from jax import lax
import numpy as np

D_MODEL = 1024
BATCH = 8
SEQ = 8192
DEPTH = 4

GRID_W = 64
CTX_LEN = 256
N_MIXERS = 3
SC_WIDTH = 3
CF_KERNEL = 31
NA_HEADS = 16
NA_HEAD_DIM = D_MODEL // NA_HEADS
NA_WIN_R = 8
NA_WIN_C = 16
FFN_DIM = ((8 * D_MODEL // 3 + 255) // 256) * 256
N_EXPERTS = 8
TOP_K = 2
EXPERT_DIM = 7 * D_MODEL // 2
N_MOD = 6
EPS = 1e-6

kernel_name = 'hybrid_dit_conv_natten_moe'


def rms_norm(x, g):
    xf = x.astype(jnp.float32)
    y = xf * lax.rsqrt(jnp.mean(xf * xf, axis=-1, keepdims=True) + EPS)
    return (y * g.astype(jnp.float32)).astype(x.dtype)


def layer_norm(x, g, b):
    xf = x.astype(jnp.float32)
    mu = jnp.mean(xf, axis=-1, keepdims=True)
    var = jnp.mean(jnp.square(xf - mu), axis=-1, keepdims=True)
    y = (xf - mu) * lax.rsqrt(var + EPS)
    return (y * g.astype(jnp.float32) + b.astype(jnp.float32)).astype(x.dtype)


def modulate(h, shift, scale):
    return h * (1 + scale) + shift


def depthwise_conv(h, w):
    k = w.shape[0]
    return lax.conv_general_dilated(
        h, w[:, None, :].astype(h.dtype), window_strides=(1,), padding=[(k // 2, k // 2)],
        dimension_numbers=('NWC', 'WIO', 'NWC'), feature_group_count=h.shape[-1])


def short_conv_mixer(h, w_in, conv_w, w_out):
    b_gate, c_gate, v = jnp.split(h @ w_in, 3, axis=-1)
    return (b_gate * depthwise_conv(c_gate * v, conv_w)) @ w_out


def conformer_conv(h, w1, b1, dw, dw_b, ln_g, ln_b, w2, b2):
    a, g = jnp.split(h @ w1 + b1, 2, axis=-1)
    u = a * jax.nn.sigmoid(g)
    u = depthwise_conv(u, dw) + dw_b
    u = jax.nn.silu(layer_norm(u, ln_g, ln_b))
    return u @ w2 + b2


def split_heads(t):
    return t.reshape(t.shape[:-1] + (NA_HEADS, NA_HEAD_DIM))


def neighborhood_attention(hx, hc, w_qkv, q_g, k_g, rpb, w_out, with_ctx_out):
    bn, seq_len, d = hx.shape
    rows = seq_len // GRID_W
    win_r = min(NA_WIN_R, rows)
    grid = (bn, rows, GRID_W, NA_HEADS, NA_HEAD_DIM)
    q, k, v = jnp.split(hx @ w_qkv, 3, axis=-1)
    q = rms_norm(split_heads(q), q_g).reshape(grid)
    k = rms_norm(split_heads(k), k_g).reshape(grid)
    v = split_heads(v).reshape(grid)
    qc, kc, vc = jnp.split(hc @ w_qkv, 3, axis=-1)
    kc = rms_norm(split_heads(kc), k_g)
    vc = split_heads(vc)
    scale = NA_HEAD_DIM ** -0.5

    row_start = jnp.clip(jnp.arange(rows) - win_r // 2, 0, rows - win_r)
    col_start = jnp.clip(jnp.arange(GRID_W) - NA_WIN_C // 2, 0, GRID_W - NA_WIN_C)
    col_idx = col_start[:, None] + jnp.arange(NA_WIN_C)[None, :]
    col_rel = col_idx - jnp.arange(GRID_W)[:, None] + (NA_WIN_C - 1)
    rpb_cols = rpb[:, :, col_rel]
    n_loc = win_r * NA_WIN_C

    def one_row(args):
        q_r, r, rs = args
        kb = lax.dynamic_slice_in_dim(k, rs, win_r, axis=1)
        vb = lax.dynamic_slice_in_dim(v, rs, win_r, axis=1)
        kw = kb[:, :, col_idx]
        vw = vb[:, :, col_idx]
        s_loc = jnp.einsum('bqhd,brqjhd->bhqrj', q_r, kw).astype(jnp.float32) * scale
        rel_r = rs + jnp.arange(win_r) - r + (NA_WIN_R - 1)
        bias = jnp.take(rpb_cols, rel_r, axis=1)
        s_loc = s_loc + jnp.transpose(bias, (0, 2, 1, 3)).astype(jnp.float32)[None]
        s_ctx = jnp.einsum('bqhd,bchd->bhqc', q_r, kc).astype(jnp.float32) * scale
        s = jnp.concatenate([s_loc.reshape(bn, NA_HEADS, GRID_W, n_loc), s_ctx], axis=-1)
        p = jax.nn.softmax(s, axis=-1).astype(v.dtype)
        p_loc = p[..., :n_loc].reshape(bn, NA_HEADS, GRID_W, win_r, NA_WIN_C)
        return (jnp.einsum('bhqrj,brqjhd->bqhd', p_loc, vw)
                + jnp.einsum('bhqc,bchd->bqhd', p[..., n_loc:], vc))

    o = lax.map(one_row, (jnp.moveaxis(q, 1, 0), jnp.arange(rows), row_start))
    y = jnp.moveaxis(o, 0, 1).reshape(bn, seq_len, d) @ w_out
    yc = None
    if with_ctx_out:
        qc = rms_norm(split_heads(qc), q_g)
        sc = jnp.einsum('bqhd,bkhd->bhqk', qc, kc).astype(jnp.float32) * scale
        pc = jax.nn.softmax(sc, axis=-1).astype(vc.dtype)
        oc = jnp.einsum('bhqk,bkhd->bqhd', pc, vc)
        yc = oc.reshape(hc.shape) @ w_out
    return y, yc


def swiglu(h, w_gate, w_up, w_down):
    return (jax.nn.silu(h @ w_gate) * (h @ w_up)) @ w_down


def moe_swiglu(h, w_router, w_gate, w_up, w_down):
    logits = (h @ w_router).astype(jnp.float32)
    top_v, top_i = lax.top_k(logits, TOP_K)
    top_p = jax.nn.softmax(top_v, axis=-1)
    combine = jnp.sum(jax.nn.one_hot(top_i, N_EXPERTS, dtype=jnp.float32) * top_p[..., None], axis=-2)
    combine = combine.astype(h.dtype)
    out = jnp.zeros_like(h)
    for e in range(N_EXPERTS):
        out = out + combine[..., e:e + 1] * swiglu(h, w_gate[e], w_up[e], w_down[e])
    return out


def setup_inputs(seed: int = 0) -> dict:
    key = jax.random.key(seed)
    ks = iter(jax.random.split(key, 40))
    d = D_MODEL
    n_a = sum(1 for i in range(DEPTH) if i % N_MIXERS == 0)
    n_b = sum(1 for i in range(DEPTH) if i % N_MIXERS == 1)
    n_c = sum(1 for i in range(DEPTH) if i % N_MIXERS == 2)
    n_dense = (DEPTH + 1) // 2
    n_moe = DEPTH // 2

    def nrm(shape, s):
        return jax.random.normal(next(ks), shape, jnp.float32) * s

    def gain(shape):
        return 1.0 + nrm(shape, 0.1)

    return {
        'x': nrm((BATCH, SEQ, d), 1.0),
        'c': nrm((BATCH, d), 1.0),
        'ctx': nrm((BATCH, CTX_LEN, d), 1.0),
        'c_ctx': nrm((d,), 1.0),
        'ada_w': nrm((DEPTH, d, N_MOD * d), 0.5 * d ** -0.5),
        'ada_b': nrm((DEPTH, N_MOD * d), 0.02),
        'norm_mix_g': gain((DEPTH, d)),
        'norm_ffn_g': gain((DEPTH, d)),
        'sc_w_in': nrm((n_a, d, 3 * d), d ** -0.5),
        'sc_conv_w': nrm((n_a, SC_WIDTH, d), SC_WIDTH ** -0.5),
        'sc_w_out': nrm((n_a, d, d), d ** -0.5),
        'cf_w1': nrm((n_b, d, 2 * d), d ** -0.5),
        'cf_b1': nrm((n_b, 2 * d), 0.02),
        'cf_dw': nrm((n_b, CF_KERNEL, d), CF_KERNEL ** -0.5),
        'cf_dw_b': nrm((n_b, d), 0.02),
        'cf_ln_g': gain((n_b, d)),
        'cf_ln_b': nrm((n_b, d), 0.02),
        'cf_w2': nrm((n_b, d, d), d ** -0.5),
        'cf_b2': nrm((n_b, d), 0.02),
        'na_w_qkv': nrm((n_c, d, 3 * d), d ** -0.5),
        'na_q_g': gain((n_c, NA_HEAD_DIM)),
        'na_k_g': gain((n_c, NA_HEAD_DIM)),
        'na_rpb': nrm((n_c, NA_HEADS, 2 * NA_WIN_R - 1, 2 * NA_WIN_C - 1), 0.5),
        'na_w_out': nrm((n_c, d, d), d ** -0.5),
        'ffn_w_gate': nrm((n_dense, d, FFN_DIM), d ** -0.5),
        'ffn_w_up': nrm((n_dense, d, FFN_DIM), d ** -0.5),
        'ffn_w_down': nrm((n_dense, FFN_DIM, d), FFN_DIM ** -0.5),
        'moe_router': nrm((n_moe, d, N_EXPERTS), d ** -0.5),
        'moe_w_gate': nrm((n_moe, N_EXPERTS, d, EXPERT_DIM), d ** -0.5),
        'moe_w_up': nrm((n_moe, N_EXPERTS, d, EXPERT_DIM), d ** -0.5),
        'moe_w_down': nrm((n_moe, N_EXPERTS, EXPERT_DIM, d), EXPERT_DIM ** -0.5),
    }


def reference(x, c, ctx, c_ctx, ada_w, ada_b, norm_mix_g, norm_ffn_g, sc_w_in, sc_conv_w, sc_w_out,
              cf_w1, cf_b1, cf_dw, cf_dw_b, cf_ln_g, cf_ln_b, cf_w2, cf_b2,
              na_w_qkv, na_q_g, na_k_g, na_rpb, na_w_out,
              ffn_w_gate, ffn_w_up, ffn_w_down, moe_router, moe_w_gate, moe_w_up, moe_w_down):
    silu_c = jax.nn.silu(c)
    silu_cc = jax.nn.silu(c_ctx)
    readers = [i for i in range(DEPTH) if i % N_MIXERS == 2]
    last_reader = readers[-1] if readers else -1
    for i in range(DEPTH):
        m = i % N_MIXERS
        j = i // N_MIXERS
        f = i // 2
        upd = i < last_reader
        mx = jnp.split((silu_c @ ada_w[i] + ada_b[i])[:, None, :], N_MOD, axis=-1)
        hx = modulate(rms_norm(x, norm_mix_g[i]), mx[0], mx[1])
        if upd or m == 2:
            mc = jnp.split(silu_cc @ ada_w[i] + ada_b[i], N_MOD, axis=-1)
            hc = modulate(rms_norm(ctx, norm_mix_g[i]), mc[0], mc[1])
        if m == 0:
            yx = short_conv_mixer(hx, sc_w_in[j], sc_conv_w[j], sc_w_out[j])
            yc = short_conv_mixer(hc, sc_w_in[j], sc_conv_w[j], sc_w_out[j]) if upd else None
        elif m == 1:
            yx = conformer_conv(hx, cf_w1[j], cf_b1[j], cf_dw[j], cf_dw_b[j], cf_ln_g[j], cf_ln_b[j],
                                cf_w2[j], cf_b2[j])
            yc = conformer_conv(hc, cf_w1[j], cf_b1[j], cf_dw[j], cf_dw_b[j], cf_ln_g[j], cf_ln_b[j],
                                cf_w2[j], cf_b2[j]) if upd else None
        else:
            yx, yc = neighborhood_attention(hx, hc, na_w_qkv[j], na_q_g[j], na_k_g[j], na_rpb[j],
                                            na_w_out[j], upd)
        x = x + mx[2] * yx
        if upd:
            ctx = ctx + mc[2] * yc
        hx = modulate(rms_norm(x, norm_ffn_g[i]), mx[3], mx[4])
        if i % 2 == 0:
            x = x + mx[5] * swiglu(hx, ffn_w_gate[f], ffn_w_up[f], ffn_w_down[f])
        else:
            x = x + mx[5] * moe_swiglu(hx, moe_router[f], moe_w_gate[f], moe_w_up[f], moe_w_down[f])
        if upd:
            hc = modulate(rms_norm(ctx, norm_ffn_g[i]), mc[3], mc[4])
            if i % 2 == 0:
                ctx = ctx + mc[5] * swiglu(hc, ffn_w_gate[f], ffn_w_up[f], ffn_w_down[f])
            else:
                ctx = ctx + mc[5] * moe_swiglu(hc, moe_router[f], moe_w_gate[f], moe_w_up[f], moe_w_down[f])
    return x
```

```python
import functools

import jax
import jax.numpy as jnp
import numpy as np
from jax import lax
from jax.experimental import pallas as pl
from jax.experimental.pallas import tpu as pltpu

GRID_W = 64
WIN_R = 8
WIN_C = 16
N_MIXERS = 3
N_MOD = 6
EPS = 1e-6

LANES = 128
SUBLANES = 8
MXU_DIM = 256
VMEM_LIMIT_BYTES = 56 * 1024 * 1024

HALO = 16
Q_ROWS = 4
NEG = -1e30
MOD_ROWS = 8

F32 = jnp.float32
BF16 = jnp.bfloat16


def _params(*sem):
    return pltpu.CompilerParams(dimension_semantics=sem, vmem_limit_bytes=VMEM_LIMIT_BYTES)


def _tile(n, pref):
    t = min(n, pref)
    while n % t or (t % HALO and t != n):
        t -= 1
    return t


def _dot(a, b):
    return jnp.dot(a, b, preferred_element_type=F32)


def _norm_mod(x, g, shift, scale):
    ms = jnp.mean(x * x, axis=-1, keepdims=True)
    return (x * lax.rsqrt(ms + EPS) * g) * (1.0 + scale) + shift


def _silu(x):
    return x * jax.nn.sigmoid(x)


def _lane_chunks(x):
    return [x[:, c * LANES:(c + 1) * LANES] for c in range(x.shape[1] // LANES)]


def _ada_kernel(c_ref, w_ref, b_ref, o_ref):
    o_ref[0] = jnp.dot(_silu(c_ref[...]), w_ref[0], precision=lax.Precision.HIGHEST,
                       preferred_element_type=F32) + b_ref[0]


def _ada(cvec, ada_w, ada_b):
    depth, d, nd = ada_w.shape
    rb = cvec.shape[0]
    tn = _tile(nd, 1024)
    return pl.pallas_call(
        _ada_kernel,
        grid=(depth, nd // tn),
        in_specs=[pl.BlockSpec((rb, d), lambda i, j: (0, 0)),
                  pl.BlockSpec((1, d, tn), lambda i, j: (i, 0, j)),
                  pl.BlockSpec((1, 1, tn), lambda i, j: (i, 0, j))],
        out_specs=pl.BlockSpec((1, rb, tn), lambda i, j: (i, 0, j)),
        out_shape=jax.ShapeDtypeStruct((depth, rb, nd), F32),
        compiler_params=_params("arbitrary", "arbitrary"),
        name="ada",
    )(cvec, ada_w, ada_b.reshape(depth, 1, nd))


def _head_norm(t, gain, bd_ref, hd):
    tt = t * t
    hi = tt.astype(BF16)
    lo = (tt - hi.astype(F32)).astype(BF16)
    bd = bd_ref[...]
    w = bd.shape[0]
    parts = []
    for c in range(t.shape[1] // w):
        sl = slice(c * w, (c + 1) * w)
        parts.append(_dot(hi[:, sl], bd) + _dot(lo[:, sl], bd))
    ss = jnp.concatenate(parts, axis=1) if len(parts) > 1 else parts[0]
    return t * lax.rsqrt(ss * (1.0 / hd) + EPS) * gain


def _in_kernel(mode, tn, hd, x_ref, mod_ref, g_ref, w_ref, *rest):
    mod = mod_ref[0]
    hx = _norm_mod(x_ref[0], g_ref[...], mod[0:1], mod[1:2]).astype(BF16)
    d = hx.shape[1]

    def mm(part, j):
        lo = part * d + j * tn
        return _dot(hx, w_ref[:, lo:lo + tn])

    for j in range(d // tn):
        sl = slice(j * tn, (j + 1) * tn)
        if mode == "sc":
            b_out, u_out = rest
            b_out[0, :, sl] = mm(0, j).astype(BF16)
            u_out[0, :, sl] = (mm(1, j) * mm(2, j)).astype(BF16)
        elif mode == "cf":
            bias_ref, u_out = rest
            a = mm(0, j) + bias_ref[:, sl]
            gte = mm(1, j) + bias_ref[:, d + j * tn:d + (j + 1) * tn]
            u_out[0, :, sl] = (a * jax.nn.sigmoid(gte)).astype(BF16)
        else:
            qg_ref, kg_ref, bd_ref, q_out, k_out, v_out = rest
            q = _head_norm(mm(0, j), qg_ref[:, sl], bd_ref, hd).astype(BF16)
            k = _head_norm(mm(1, j), kg_ref[:, sl], bd_ref, hd).astype(BF16)
            v = mm(2, j).astype(BF16)
            for c in range(tn // LANES):
                pair = j * (tn // LANES) + c
                q_out[0, pair] = q[:, c * LANES:(c + 1) * LANES]
                k_out[0, pair] = k[:, c * LANES:(c + 1) * LANES]
                v_out[0, pair] = v[:, c * LANES:(c + 1) * LANES]


def _mixer_in(mode, x, mod, g, w, extras, n_out, hd=1, tm_pref=512):
    b, l, d = x.shape
    tm = _tile(l, tm_pref)
    tn = _tile(d, 512)
    full = lambda a: pl.BlockSpec(a.shape, lambda bi, i: (0,) * a.ndim)
    tok = pl.BlockSpec((1, tm, d), lambda bi, i: (bi, i, 0))
    if mode == "qkv":
        out_spec = pl.BlockSpec((1, d // LANES, tm, LANES), lambda bi, i: (bi, 0, i, 0))
        out_shape = jax.ShapeDtypeStruct((b, d // LANES, l, LANES), BF16)
    else:
        out_spec, out_shape = tok, jax.ShapeDtypeStruct((b, l, d), BF16)
    return pl.pallas_call(
        functools.partial(_in_kernel, mode, tn, hd),
        grid=(b, l // tm),
        in_specs=[tok, pl.BlockSpec((1, MOD_ROWS, d), lambda bi, i: (bi, 0, 0)), full(g), full(w)]
                 + [full(e) for e in extras],
        out_specs=[out_spec] * n_out,
        out_shape=[out_shape] * n_out,
        compiler_params=_params("parallel", "parallel"),
        name="mixer_in_" + mode,
    )(x, mod, g, w, *extras)


def _with_halo(u_ref, up_ref, un_ref):
    i = pl.program_id(1)
    has_prev = jnp.where(i > 0, 1.0, 0.0)
    has_next = jnp.where(i < pl.num_programs(1) - 1, 1.0, 0.0)
    return jnp.concatenate([up_ref[0].astype(F32) * has_prev, u_ref[0].astype(F32),
                            un_ref[0].astype(F32) * has_next], axis=0)


def _sc_out_kernel(x_ref, mod_ref, b_ref, u_ref, up_ref, un_ref, cw_ref, w_ref, o_ref):
    ext = _with_halo(u_ref, up_ref, un_ref)
    n = ext.shape[0]
    tm = n - 2 * HALO
    taps = cw_ref.shape[0]
    conv = None
    for k in range(taps):
        off = k - taps // 2
        sh = ext if off == 0 else pltpu.roll(ext, (-off) % n, axis=0)
        term = cw_ref[k:k + 1, :] * sh[HALO:HALO + tm]
        conv = term if conv is None else conv + term
    z = (b_ref[0].astype(F32) * conv).astype(BF16)
    o_ref[0] = x_ref[0] + mod_ref[0][2:3] * _dot(z, w_ref[...])


def _cf_out_kernel(taps, x_ref, mod_ref, u_ref, up_ref, un_ref, dw_ref, dwb_ref, lng_ref, lnb_ref,
                   w_ref, b2_ref, o_ref):
    ext = _with_halo(u_ref, up_ref, un_ref)
    n = ext.shape[0]
    tm = n - 2 * HALO
    base = HALO - taps // 2
    acc = None
    for s in range(SUBLANES):
        sh = ext if s == 0 else pltpu.roll(ext, n - s, axis=0)
        for p in range((base + taps - 1) // SUBLANES + 1):
            k = SUBLANES * p + s - base
            if 0 <= k < taps:
                term = dw_ref[k:k + 1, :] * sh[SUBLANES * p:SUBLANES * p + tm]
                acc = term if acc is None else acc + term
    u2 = acc + dwb_ref[...]
    mu = jnp.mean(u2, axis=-1, keepdims=True)
    cen = u2 - mu
    var = jnp.mean(cen * cen, axis=-1, keepdims=True)
    y = cen * lax.rsqrt(var + EPS) * lng_ref[...] + lnb_ref[...]
    y = _dot(_silu(y).astype(BF16), w_ref[...]) + b2_ref[...]
    o_ref[0] = x_ref[0] + mod_ref[0][2:3] * y


def _mixer_out(body, name, x, mod, toks, halo_of, consts, tm_pref=512):
    b, l, d = x.shape
    tm = _tile(l, tm_pref)
    per = tm // HALO
    last = l // HALO - 1
    tok = pl.BlockSpec((1, tm, d), lambda bi, i: (bi, i, 0))
    full = lambda a: pl.BlockSpec(a.shape, lambda bi, i: (0,) * a.ndim)
    in_specs = ([tok, pl.BlockSpec((1, MOD_ROWS, d), lambda bi, i: (bi, 0, 0))] + [tok] * len(toks)
                + [pl.BlockSpec((1, HALO, d), lambda bi, i: (bi, jnp.maximum(i * per - 1, 0), 0)),
                   pl.BlockSpec((1, HALO, d), lambda bi, i: (bi, jnp.minimum((i + 1) * per, last), 0))]
                + [full(c) for c in consts])
    return pl.pallas_call(
        body,
        grid=(b, l // tm),
        in_specs=in_specs,
        out_specs=tok,
        out_shape=jax.ShapeDtypeStruct((b, l, d), F32),
        compiler_params=_params("parallel", "parallel"),
        name=name,
    )(x, mod, *toks, halo_of, halo_of, *consts)


def _attn_out_kernel(x_ref, mod_ref, a_ref, w_ref, o_ref):
    a = jnp.concatenate([a_ref[0, p] for p in range(a_ref.shape[1])], axis=1)
    o_ref[0] = x_ref[0] + mod_ref[0][2:3] * _dot(a, w_ref[...])


def _attn_out(x, mod, a, w, tm_pref=512):
    b, l, d = x.shape
    tm = _tile(l, tm_pref)
    tok = pl.BlockSpec((1, tm, d), lambda bi, i: (bi, i, 0))
    return pl.pallas_call(
        _attn_out_kernel,
        grid=(b, l // tm),
        in_specs=[tok, pl.BlockSpec((1, MOD_ROWS, d), lambda bi, i: (bi, 0, 0)),
                  pl.BlockSpec((1, d // LANES, tm, LANES), lambda bi, i: (bi, 0, i, 0)),
                  pl.BlockSpec(w.shape, lambda bi, i: (0, 0))],
        out_specs=tok,
        out_shape=jax.ShapeDtypeStruct((b, l, d), F32),
        compiler_params=_params("parallel", "parallel"),
        name="attn_out",
    )(x, mod, a, w)


def _swiglu_step(hx, wg_ref, wu_ref, wd_ref, acc_s, f):
    g = _dot(hx, wg_ref[...])
    u = _dot(hx, wu_ref[...])
    part = _dot((_silu(g) * u).astype(BF16), wd_ref[...])

    @pl.when(f == 0)
    def _():
        acc_s[...] = part

    @pl.when(f > 0)
    def _():
        acc_s[...] += part


def _ffn_kernel(x_ref, mod_ref, g_ref, wg_ref, wu_ref, wd_ref, o_ref, hx_s, acc_s):
    f = pl.program_id(2)

    @pl.when(f == 0)
    def _():
        mod = mod_ref[0]
        hx_s[...] = _norm_mod(x_ref[0], g_ref[...], mod[3:4], mod[4:5]).astype(BF16)

    _swiglu_step(hx_s[...], wg_ref, wu_ref, wd_ref, acc_s, f)

    @pl.when(f == pl.num_programs(2) - 1)
    def _():
        o_ref[0] = x_ref[0] + mod_ref[0][5:6] * acc_s[...]


def _ffn_chunk(fdim, pref):
    best = LANES
    for k in range(1, fdim // LANES + 1):
        t = k * LANES
        if fdim % t == 0 and t <= pref:
            best = t
    return best


def _ffn(x, mod, g, wg, wu, wd, tm_pref=512, tf_pref=1408):
    b, l, d = x.shape
    fdim = wg.shape[1]
    tm = _tile(l, tm_pref)
    tf = _ffn_chunk(fdim, tf_pref)
    tok = pl.BlockSpec((1, tm, d), lambda bi, i, f: (bi, i, 0))
    return pl.pallas_call(
        _ffn_kernel,
        grid=(b, l // tm, fdim // tf),
        in_specs=[tok, pl.BlockSpec((1, MOD_ROWS, d), lambda bi, i, f: (bi, 0, 0)),
                  pl.BlockSpec(g.shape, lambda bi, i, f: (0, 0)),
                  pl.BlockSpec((d, tf), lambda bi, i, f: (0, f)),
                  pl.BlockSpec((d, tf), lambda bi, i, f: (0, f)),
                  pl.BlockSpec((tf, d), lambda bi, i, f: (f, 0))],
        out_specs=tok,
        out_shape=jax.ShapeDtypeStruct((b, l, d), F32),
        scratch_shapes=[pltpu.VMEM((tm, d), BF16), pltpu.VMEM((tm, d), F32)],
        compiler_params=_params("parallel", "parallel", "arbitrary"),
        name="ffn",
    )(x, mod, g, wg, wu, wd)


def _to_token_tiled(ref, x):
    s = x.shape[1] // LANES
    for c, chunk in enumerate(_lane_chunks(x)):
        ref[pl.ds(c, x.shape[0], stride=s), :] = chunk


def _from_token_tiled(ref, tokens):
    s = ref.shape[0] // tokens
    return jnp.concatenate([ref[pl.ds(c, tokens, stride=s), :] for c in range(s)], axis=1)


def _router_kernel(n_exp, x_ref, mod_ref, g_ref, wr_ref, hx_ref, meta_ref, cnt_ref, carry_s):
    @pl.when((pl.program_id(0) == 0) & (pl.program_id(1) == 0))
    def _():
        carry_s[...] = jnp.zeros_like(carry_s)

    mod = mod_ref[0]
    hx = _norm_mod(x_ref[0], g_ref[...], mod[3:4], mod[4:5])
    _to_token_tiled(hx_ref, hx)
    tm = hx.shape[0]
    logits = jnp.dot(hx, wr_ref[...], precision=lax.Precision.HIGHEST, preferred_element_type=F32)
    lane = lax.broadcasted_iota(jnp.int32, (tm, LANES), 1).astype(F32)
    logits = jnp.where(lane < n_exp, logits, -jnp.inf)
    m1 = jnp.max(logits, axis=-1, keepdims=True)
    i1 = jnp.min(jnp.where(logits == m1, lane, float(LANES)), axis=-1, keepdims=True)
    rest = jnp.where(lane == i1, -jnp.inf, logits)
    m2 = jnp.max(rest, axis=-1, keepdims=True)
    i2 = jnp.min(jnp.where(rest == m2, lane, float(LANES)), axis=-1, keepdims=True)
    e = jnp.exp(m2 - m1)
    p1 = 1.0 / (1.0 + e)
    p2 = e / (1.0 + e)
    oh1 = lane == i1
    oh2 = lane == i2
    a = jnp.where(oh1 | oh2, 1.0, 0.0)
    below = jnp.where(lax.broadcasted_iota(jnp.int32, (tm, tm), 1)
                      < lax.broadcasted_iota(jnp.int32, (tm, tm), 0), 1.0, 0.0).astype(BF16)
    pre = _dot(below, a.astype(BF16)) + carry_s[0:1, :]
    r1 = jnp.sum(jnp.where(oh1, pre, 0.0), axis=-1, keepdims=True)
    r2 = jnp.sum(jnp.where(oh2, pre, 0.0), axis=-1, keepdims=True)
    carry_s[...] = carry_s[...] + jnp.sum(a, axis=0, keepdims=True)
    cnt_ref[...] = carry_s[...]
    meta = jnp.zeros((tm, LANES), F32)
    for col, val in enumerate((i1, i2, r1, r2, p1, p2)):
        meta = jnp.where(lane == col, val, meta)
    meta_ref[0] = meta


def _dispatch_kernel(s, te, zt_ref, nz_ref, pos_ref, hx_ref, xs_hbm, zero_s, sem):
    td = hx_ref.shape[0] // s

    @pl.when(pl.program_id(0) == 0)
    def _():
        zero_s[...] = jnp.zeros_like(zero_s)

        def clear(z, carry):
            dst = pl.multiple_of(zt_ref[z] * (te * s), te * s)
            cp = pltpu.make_async_copy(zero_s, xs_hbm.at[pl.ds(dst, te * s)], sem)
            cp.start()
            cp.wait()
            return carry

        lax.fori_loop(0, nz_ref[0], clear, 0)

    def row_copy(t, slot):
        dst = pl.multiple_of(pos_ref[0, 0, 2 * t + slot] * s, s)
        return pltpu.make_async_copy(hx_ref.at[pl.ds(pl.multiple_of(t * s, s), s)], xs_hbm.at[pl.ds(dst, s)], sem)

    def issue(t, carry):
        row_copy(t, 0).start()
        row_copy(t, 1).start()
        return carry

    def drain(t, carry):
        row_copy(t, 0).wait()
        row_copy(t, 1).wait()
        return carry

    lax.fori_loop(0, td, issue, 0)
    lax.fori_loop(0, td, drain, 0)


def _expert_kernel(te, te_ref, tv_ref, xs_ref, wg_ref, wu_ref, wd_ref, ys_ref, hx_s, acc_s):
    del te_ref
    f = pl.program_id(1)
    last = pl.num_programs(1) - 1
    valid = tv_ref[pl.program_id(0)]

    @pl.when(valid > 0)
    def _():
        @pl.when(f == 0)
        def _():
            hx_s[...] = _from_token_tiled(xs_ref, te).astype(BF16)

        _swiglu_step(hx_s[...], wg_ref.at[0], wu_ref.at[0], wd_ref.at[0], acc_s, f)

        @pl.when(f == last)
        def _():
            _to_token_tiled(ys_ref, acc_s[...])

    @pl.when((valid == 0) & (f == last))
    def _():
        ys_ref[...] = jnp.zeros_like(ys_ref)


def _combine_kernel(s, pos_ref, x_ref, meta_ref, mod_ref, ys_hbm, o_ref, y1_s, y2_s, sem):
    tc = x_ref.shape[0]

    def row_copy(t, slot):
        src = pl.multiple_of(pos_ref[0, 0, 2 * t + slot] * s, s)
        dst = y1_s if slot == 0 else y2_s
        return pltpu.make_async_copy(ys_hbm.at[pl.ds(src, s)], dst.at[pl.ds(pl.multiple_of(t * s, s), s)], sem)

    def issue(t, carry):
        row_copy(t, 0).start()
        row_copy(t, 1).start()
        return carry

    def drain(t, carry):
        row_copy(t, 0).wait()
        row_copy(t, 1).wait()
        return carry

    lax.fori_loop(0, tc, issue, 0)
    lax.fori_loop(0, tc, drain, 0)
    meta = meta_ref[...]
    y = meta[:, 4:5] * _from_token_tiled(y1_s, tc) + meta[:, 5:6] * _from_token_tiled(y2_s, tc)
    o_ref[...] = x_ref[...] + mod_ref[0][5:6] * y


def _moe(x, mod, g, w_router, wg, wu, wd, tm_pref=512, tf_pref=1792):
    b, l, d = x.shape
    n = b * l
    s = d // LANES
    n_exp, _, fdim = wg.shape
    tm = _tile(l, tm_pref)
    te = _tile(2 * n, tm_pref)
    tf = _ffn_chunk(fdim, tf_pref)
    wr = jnp.zeros((d, LANES), F32).at[:, :n_exp].set(w_router)
    per_b = l // tm

    hx, meta, cnt = pl.pallas_call(
        functools.partial(_router_kernel, n_exp),
        grid=(b, per_b),
        in_specs=[pl.BlockSpec((1, tm, d), lambda bi, i: (bi, i, 0)),
                  pl.BlockSpec((1, MOD_ROWS, d), lambda bi, i: (bi, 0, 0)),
                  pl.BlockSpec(g.shape, lambda bi, i: (0, 0)),
                  pl.BlockSpec(wr.shape, lambda bi, i: (0, 0))],
        out_specs=[pl.BlockSpec((tm * s, LANES), lambda bi, i: (bi * per_b + i, 0)),
                   pl.BlockSpec((1, tm, LANES), lambda bi, i: (bi, i, 0)),
                   pl.BlockSpec((SUBLANES, LANES), lambda bi, i: (0, 0))],
        out_shape=[jax.ShapeDtypeStruct((n * s, LANES), F32), jax.ShapeDtypeStruct((b, l, LANES), F32),
                   jax.ShapeDtypeStruct((SUBLANES, LANES), F32)],
        scratch_shapes=[pltpu.VMEM((SUBLANES, LANES), F32)],
        compiler_params=_params("arbitrary", "arbitrary"),
        name="moe_router",
    )(x, mod, g, wr)

    meta = meta.reshape(n, LANES)
    ids = meta[:, 0:2].astype(jnp.int32)
    rank = meta[:, 2:4].astype(jnp.int32)
    counts = cnt[0, :n_exp].astype(jnp.int32)
    tiles = (counts + te - 1) // te
    tile_end = jnp.cumsum(tiles)
    tile_start = tile_end - tiles
    pos = (jnp.take(tile_start * te, ids) + rank).reshape(n // tm, 1, 2 * tm)
    n_tiles = (2 * n) // te + n_exp
    r = jnp.arange(n_tiles, dtype=jnp.int32)
    tile_e = jnp.minimum(jnp.sum((r[:, None] >= tile_end[None, :]).astype(jnp.int32), axis=1), n_exp - 1)
    tile_valid = jnp.clip(jnp.take(counts, tile_e) - (r - jnp.take(tile_start, tile_e)) * te, 0, te)
    tile_valid = jnp.where(r < tile_end[-1], tile_valid, 0).astype(jnp.int32)
    rows = n_tiles * te

    cand = jnp.concatenate([jnp.where(tiles > 0, tile_end - 1, n_tiles),
                            tile_end[-1] + jnp.arange(n_exp, dtype=jnp.int32)])
    zero_tiles = jnp.sort(jnp.minimum(cand, n_tiles)).astype(jnp.int32)
    n_zero = jnp.sum((zero_tiles < n_tiles).astype(jnp.int32)).reshape(1)

    xs = pl.pallas_call(
        functools.partial(_dispatch_kernel, s, te),
        grid_spec=pltpu.PrefetchScalarGridSpec(
            num_scalar_prefetch=2,
            grid=(n // tm,),
            in_specs=[pl.BlockSpec((1, 1, 2 * tm), lambda i, zt, nz: (i, 0, 0), memory_space=pltpu.SMEM),
                      pl.BlockSpec((tm * s, LANES), lambda i, zt, nz: (i, 0))],
            out_specs=pl.BlockSpec(memory_space=pl.ANY),
            scratch_shapes=[pltpu.VMEM((te * s, LANES), F32), pltpu.SemaphoreType.DMA(())]),
        out_shape=jax.ShapeDtypeStruct((rows * s, LANES), F32),
        compiler_params=_params("arbitrary"),
        name="moe_dispatch",
    )(zero_tiles, n_zero, pos, hx)

    ys = pl.pallas_call(
        functools.partial(_expert_kernel, te),
        grid_spec=pltpu.PrefetchScalarGridSpec(
            num_scalar_prefetch=2,
            grid=(n_tiles, fdim // tf),
            in_specs=[pl.BlockSpec((te * s, LANES), lambda r, f, e, v: (r, 0)),
                      pl.BlockSpec((1, d, tf), lambda r, f, e, v: (e[r], 0, f)),
                      pl.BlockSpec((1, d, tf), lambda r, f, e, v: (e[r], 0, f)),
                      pl.BlockSpec((1, tf, d), lambda r, f, e, v: (e[r], f, 0))],
            out_specs=pl.BlockSpec((te * s, LANES), lambda r, f, e, v: (r, 0)),
            scratch_shapes=[pltpu.VMEM((te, d), BF16), pltpu.VMEM((te, d), F32)]),
        out_shape=jax.ShapeDtypeStruct((rows * s, LANES), F32),
        compiler_params=_params("parallel", "arbitrary"),
        name="moe_experts",
    )(tile_e, tile_valid, xs, wg, wu, wd)

    flat = pl.BlockSpec((tm, d), lambda i: (i, 0))
    out = pl.pallas_call(
        functools.partial(_combine_kernel, s),
        grid=(n // tm,),
        in_specs=[pl.BlockSpec((1, 1, 2 * tm), lambda i: (i, 0, 0), memory_space=pltpu.SMEM), flat,
                  pl.BlockSpec((tm, LANES), lambda i: (i, 0)),
                  pl.BlockSpec((1, MOD_ROWS, d), lambda i: (i // per_b, 0, 0)),
                  pl.BlockSpec(memory_space=pl.ANY)],
        out_specs=flat,
        out_shape=jax.ShapeDtypeStruct((n, d), F32),
        scratch_shapes=[pltpu.VMEM((tm * s, LANES), F32), pltpu.VMEM((tm * s, LANES), F32),
                        pltpu.SemaphoreType.DMA(())],
        compiler_params=_params("arbitrary"),
        name="moe_combine",
    )(pos, x.reshape(n, d), meta, mod, ys)
    return out.reshape(b, l, d)


def _attn_kernel(q_ref, kp_ref, kc_ref, kn_ref, vp_ref, vc_ref, vn_ref, kx_ref, vx_ref, tab_ref, vm_ref, o_ref):
    hd = LANES // 2
    n_pairs, tq = q_ref.shape[1], q_ref.shape[2]
    lane = lax.broadcasted_iota(jnp.int32, (tq, LANES), 1)
    n_key_tiles = 3 * tq // LANES
    nt = (((1,), (1,)), ((), ()))

    def pair(p, carry):
        q2 = q_ref[0, p]
        k2 = jnp.concatenate([kp_ref[0, p], kc_ref[0, p], kn_ref[0, p]], axis=0)
        v2 = jnp.concatenate([vp_ref[0, p], vc_ref[0, p], vn_ref[0, p]], axis=0)
        kx2 = kx_ref[0, p]
        vx2 = vx_ref[0, p]
        halves = []
        for hh in range(2):
            qm = jnp.where((lane >= hh * hd) & (lane < (hh + 1) * hd), q2, jnp.zeros_like(q2))
            s_loc = lax.dot_general(qm, k2, nt, preferred_element_type=F32)
            s_ctx = lax.dot_general(qm, kx2, nt, preferred_element_type=F32)
            h = 2 * p + hh
            bias = jnp.concatenate(
                [jnp.concatenate([tab_ref[h, WIN_R // 2 - 1 + 2 * j - qr]
                                  + vm_ref[0, qr * n_key_tiles + j:qr * n_key_tiles + j + 1, :]
                                  for j in range(n_key_tiles)], axis=1)
                 for qr in range(Q_ROWS)], axis=0)
            s_loc = s_loc + bias
            m = jnp.maximum(jnp.max(s_loc, axis=-1, keepdims=True), jnp.max(s_ctx, axis=-1, keepdims=True))
            p_loc = jnp.exp(s_loc - m)
            p_ctx = jnp.exp(s_ctx - m)
            den = jnp.sum(p_loc, axis=-1, keepdims=True) + jnp.sum(p_ctx, axis=-1, keepdims=True)
            halves.append((_dot(p_loc.astype(BF16), v2) + _dot(p_ctx.astype(BF16), vx2)) / den)
        o_ref[0, p] = jnp.where(lane < hd, halves[0], halves[1]).astype(BF16)
        return carry

    lax.fori_loop(0, n_pairs, pair, 0)


def _attn_tables(rpb, rows):
    w = GRID_W
    qc = np.arange(w)
    cs = np.clip(qc - WIN_C // 2, 0, w - WIN_C)
    kc = np.arange(w)
    in_win = (kc[None, :] >= cs[:, None]) & (kc[None, :] < cs[:, None] + WIN_C)
    rel_c = np.clip(kc[None, :] - qc[:, None] + WIN_C - 1, 0, 2 * WIN_C - 2)
    n_m = 2 * WIN_R - 2
    rel_r = np.arange(n_m)[:, None] + np.arange(2)[None, :]
    g = rpb[:, rel_r][:, :, :, rel_c]
    g = jnp.where(in_win[None, None, None], g, NEG)
    tab = jnp.transpose(g, (0, 1, 3, 2, 4)).reshape(rpb.shape[0], n_m, w, 2 * w).astype(F32)

    key_tiles = 3 * Q_ROWS // 2
    nblk = rows // Q_ROWS
    vm = np.full((3, Q_ROWS * key_tiles, 2 * w), NEG, np.float32)
    for case, blk in enumerate((0, 1, nblk - 1)):
        r0 = blk * Q_ROWS
        for qr in range(Q_ROWS):
            rs = min(max(r0 + qr - WIN_R // 2, 0), rows - WIN_R)
            for j in range(key_tiles):
                for dr in range(2):
                    krow = r0 - Q_ROWS + 2 * j + dr
                    if rs <= krow < rs + WIN_R:
                        vm[case, qr * key_tiles + j, dr * w:(dr + 1) * w] = 0.0
    return tab, jnp.asarray(vm)


def _attention(q, k, v, kx, vx, rpb):
    b, pairs, l, _ = q.shape
    rows = l // GRID_W
    assert GRID_W * 2 == LANES and Q_ROWS == WIN_R // 2 and rows % Q_ROWS == 0 and rows // Q_ROWS >= 3
    tq = Q_ROWS * GRID_W
    nblk = rows // Q_ROWS
    ctx = kx.shape[2]
    tab, vm = _attn_tables(rpb, rows)
    blk = lambda fn: pl.BlockSpec((1, pairs, tq, LANES), fn)
    prev = blk(lambda bi, i: (bi, 0, jnp.maximum(i - 1, 0), 0))
    cur = blk(lambda bi, i: (bi, 0, i, 0))
    nxt = blk(lambda bi, i: (bi, 0, jnp.minimum(i + 1, nblk - 1), 0))
    cx = pl.BlockSpec((1, pairs, ctx, LANES), lambda bi, i: (bi, 0, 0, 0))
    case = lambda bi, i: (jnp.where(i > 0, 1, 0) + jnp.where(i == nblk - 1, 1, 0), 0, 0)
    return pl.pallas_call(
        _attn_kernel,
        grid=(b, nblk),
        in_specs=[cur, prev, cur, nxt, prev, cur, nxt, cx, cx,
                  pl.BlockSpec(tab.shape, lambda bi, i: (0, 0, 0, 0)),
                  pl.BlockSpec((1,) + vm.shape[1:], case)],
        out_specs=cur,
        out_shape=jax.ShapeDtypeStruct((b, pairs, l, LANES), BF16),
        compiler_params=_params("parallel", "parallel"),
        name="attention",
    )(q, k, k, k, v, v, v, kx, vx, tab, vm)


def _short_conv(x, mod, g, w_in, conv_w, w_out):
    b_gate, u = _mixer_in("sc", x, mod, g, w_in.astype(BF16), (), 2)
    return _mixer_out(_sc_out_kernel, "short_conv_out", x, mod, (b_gate, u), u, (conv_w, w_out.astype(BF16)))


def _conformer(x, mod, g, w1, b1, dw, dw_b, ln_g, ln_b, w2, b2):
    assert dw.shape[0] // 2 <= HALO
    row = lambda a: a.reshape(1, -1)
    (u,) = _mixer_in("cf", x, mod, g, w1.astype(BF16), (row(b1),), 1)
    body = functools.partial(_cf_out_kernel, dw.shape[0])
    return _mixer_out(body, "conformer_out", x, mod, (u,), u,
                      (dw, row(dw_b), row(ln_g), row(ln_b), w2.astype(BF16), row(b2)), tm_pref=256)


def _qkv(x, mod, g, w_qkv, q_g, k_g):
    d = x.shape[-1]
    hd = q_g.shape[0]
    heads = d // hd
    w = min(MXU_DIM, d)
    bd = jnp.asarray(np.kron(np.eye(w // hd), np.ones((hd, hd))), BF16)
    qg = (jnp.tile(q_g, heads) * hd ** -0.5).reshape(1, d)
    kg = jnp.tile(k_g, heads).reshape(1, d)
    return _mixer_in("qkv", x, mod, g, w_qkv.astype(BF16), (qg, kg, bd), 3, hd=hd)


def kernel(x, c, ctx, c_ctx, ada_w, ada_b, norm_mix_g, norm_ffn_g, sc_w_in, sc_conv_w, sc_w_out, cf_w1, cf_b1, cf_dw, cf_dw_b, cf_ln_g, cf_ln_b, cf_w2, cf_b2, na_w_qkv, na_q_g, na_k_g, na_rpb, na_w_out, ffn_w_gate, ffn_w_up, ffn_w_down, moe_router, moe_w_gate, moe_w_up, moe_w_down):
    b, l, d = x.shape
    depth = ada_w.shape[0]
    rb = -(-(b + 1) // SUBLANES) * SUBLANES
    cvec = jnp.zeros((rb, d), F32).at[:b].set(c).at[b].set(c_ctx)
    ada = _ada(cvec, ada_w, ada_b)
    pad = ((0, 0), (0, MOD_ROWS - N_MOD), (0, 0))

    readers = [i for i in range(depth) if i % N_MIXERS == 2]
    last_reader = readers[-1] if readers else -1
    for i in range(depth):
        m, j, f = i % N_MIXERS, i // N_MIXERS, i // 2
        upd = i < last_reader
        mx = jnp.pad(ada[i, :b].reshape(b, N_MOD, d), pad)
        mc = jnp.broadcast_to(jnp.pad(ada[i, b].reshape(1, N_MOD, d), pad), (b, MOD_ROWS, d))
        gm = norm_mix_g[i].reshape(1, d)
        gf = norm_ffn_g[i].reshape(1, d)
        streams = [(x, mx)] + ([(ctx, mc)] if upd else [])
        if m == 0:
            mixed = [_short_conv(s, md, gm, sc_w_in[j], sc_conv_w[j], sc_w_out[j]) for s, md in streams]
        elif m == 1:
            mixed = [_conformer(s, md, gm, cf_w1[j], cf_b1[j], cf_dw[j], cf_dw_b[j], cf_ln_g[j], cf_ln_b[j],
                                cf_w2[j], cf_b2[j]) for s, md in streams]
        else:
            assert not upd, "a context update after an attention layer is not needed for this depth"
            q, k, v = _qkv(x, mx, gm, na_w_qkv[j], na_q_g[j], na_k_g[j])
            _, kc, vc = _qkv(ctx, mc, gm, na_w_qkv[j], na_q_g[j], na_k_g[j])
            mixed = [_attn_out(x, mx, _attention(q, k, v, kc, vc, na_rpb[j]), na_w_out[j].astype(BF16))]
        outs = []
        for s, md in zip(mixed, (mx, mc)):
            if i % 2 == 0:
                outs.append(_ffn(s, md, gf, ffn_w_gate[f].astype(BF16), ffn_w_up[f].astype(BF16),
                                 ffn_w_down[f].astype(BF16)))
            else:
                outs.append(_moe(s, md, gf, moe_router[f], moe_w_gate[f].astype(BF16),
                                 moe_w_up[f].astype(BF16), moe_w_down[f].astype(BF16)))
        x = outs[0]
        if upd:
            ctx = outs[1]
    return x
```

```python
import functools

import jax
import jax.numpy as jnp
import numpy as np
from jax import lax
from jax.experimental import pallas as pl
from jax.experimental.pallas import tpu as pltpu

GRID_W = 64
WIN_R = 8
WIN_C = 16
N_MIXERS = 3
N_MOD = 6
EPS = 1e-6

LANES = 128
SUBLANES = 8
MXU_DIM = 256
VMEM_LIMIT_BYTES = 56 * 1024 * 1024

HALO = 16
Q_ROWS = 4
NEG = -1e30
MOD_ROWS = 8

F32 = jnp.float32
BF16 = jnp.bfloat16


def _params(*sem):
    return pltpu.CompilerParams(dimension_semantics=sem, vmem_limit_bytes=VMEM_LIMIT_BYTES)


def _tile(n, pref):
    t = min(n, pref)
    while n % t or (t % HALO and t != n):
        t -= 1
    return t


def _dot(a, b):
    return jnp.dot(a, b, preferred_element_type=F32)


def _norm_mod(x, g, shift, scale):
    ms = jnp.mean(x * x, axis=-1, keepdims=True)
    return (x * lax.rsqrt(ms + EPS) * g) * (1.0 + scale) + shift


def _silu(x):
    return x * jax.nn.sigmoid(x)


def _lane_chunks(x):
    return [x[:, c * LANES:(c + 1) * LANES] for c in range(x.shape[1] // LANES)]


def _ada_kernel(c_ref, w_ref, b_ref, o_ref):
    o_ref[0] = jnp.dot(_silu(c_ref[...]), w_ref[0], precision=lax.Precision.HIGHEST,
                       preferred_element_type=F32) + b_ref[0]


def _ada(cvec, ada_w, ada_b):
    depth, d, nd = ada_w.shape
    rb = cvec.shape[0]
    tn = _tile(nd, 1024)
    return pl.pallas_call(
        _ada_kernel,
        grid=(depth, nd // tn),
        in_specs=[pl.BlockSpec((rb, d), lambda i, j: (0, 0)),
                  pl.BlockSpec((1, d, tn), lambda i, j: (i, 0, j)),
                  pl.BlockSpec((1, 1, tn), lambda i, j: (i, 0, j))],
        out_specs=pl.BlockSpec((1, rb, tn), lambda i, j: (i, 0, j)),
        out_shape=jax.ShapeDtypeStruct((depth, rb, nd), F32),
        compiler_params=_params("arbitrary", "arbitrary"),
        name="ada",
    )(cvec, ada_w, ada_b.reshape(depth, 1, nd))


def _head_norm(t, gain, bd_ref, hd):
    tt = t * t
    hi = tt.astype(BF16)
    lo = (tt - hi.astype(F32)).astype(BF16)
    bd = bd_ref[...]
    w = bd.shape[0]
    parts = []
    for c in range(t.shape[1] // w):
        sl = slice(c * w, (c + 1) * w)
        parts.append(_dot(hi[:, sl], bd) + _dot(lo[:, sl], bd))
    ss = jnp.concatenate(parts, axis=1) if len(parts) > 1 else parts[0]
    return t * lax.rsqrt(ss * (1.0 / hd) + EPS) * gain


def _in_kernel(mode, tn, hd, x_ref, mod_ref, g_ref, w_ref, *rest):
    mod = mod_ref[0]
    hx = _norm_mod(x_ref[0], g_ref[...], mod[0:1], mod[1:2]).astype(BF16)
    d = hx.shape[1]

    def mm(part, j):
        lo = part * d + j * tn
        return _dot(hx, w_ref[:, lo:lo + tn])

    for j in range(d // tn):
        sl = slice(j * tn, (j + 1) * tn)
        if mode == "sc":
            b_out, u_out = rest
            b_out[0, :, sl] = mm(0, j).astype(BF16)
            u_out[0, :, sl] = (mm(1, j) * mm(2, j)).astype(BF16)
        elif mode == "cf":
            bias_ref, u_out = rest
            a = mm(0, j) + bias_ref[:, sl]
            gte = mm(1, j) + bias_ref[:, d + j * tn:d + (j + 1) * tn]
            u_out[0, :, sl] = (a * jax.nn.sigmoid(gte)).astype(BF16)
        else:
            qg_ref, kg_ref, bd_ref, q_out, k_out, v_out = rest
            q = _head_norm(mm(0, j), qg_ref[:, sl], bd_ref, hd).astype(BF16)
            k = _head_norm(mm(1, j), kg_ref[:, sl], bd_ref, hd).astype(BF16)
            v = mm(2, j).astype(BF16)
            for c in range(tn // LANES):
                pair = j * (tn // LANES) + c
                q_out[0, pair] = q[:, c * LANES:(c + 1) * LANES]
                k_out[0, pair] = k[:, c * LANES:(c + 1) * LANES]
                v_out[0, pair] = v[:, c * LANES:(c + 1) * LANES]


def _mixer_in(mode, x, mod, g, w, extras, n_out, hd=1, tm_pref=512):
    b, l, d = x.shape
    tm = _tile(l, tm_pref)
    tn = _tile(d, 512)
    full = lambda a: pl.BlockSpec(a.shape, lambda bi, i: (0,) * a.ndim)
    tok = pl.BlockSpec((1, tm, d), lambda bi, i: (bi, i, 0))
    if mode == "qkv":
        out_spec = pl.BlockSpec((1, d // LANES, tm, LANES), lambda bi, i: (bi, 0, i, 0))
        out_shape = jax.ShapeDtypeStruct((b, d // LANES, l, LANES), BF16)
    else:
        out_spec, out_shape = tok, jax.ShapeDtypeStruct((b, l, d), BF16)
    return pl.pallas_call(
        functools.partial(_in_kernel, mode, tn, hd),
        grid=(b, l // tm),
        in_specs=[tok, pl.BlockSpec((1, MOD_ROWS, d), lambda bi, i: (bi, 0, 0)), full(g), full(w)]
                 + [full(e) for e in extras],
        out_specs=[out_spec] * n_out,
        out_shape=[out_shape] * n_out,
        compiler_params=_params("parallel", "parallel"),
        name="mixer_in_" + mode,
    )(x, mod, g, w, *extras)


def _with_halo(u_ref, up_ref, un_ref):
    i = pl.program_id(1)
    has_prev = jnp.where(i > 0, 1.0, 0.0)
    has_next = jnp.where(i < pl.num_programs(1) - 1, 1.0, 0.0)
    return jnp.concatenate([up_ref[0].astype(F32) * has_prev, u_ref[0].astype(F32),
                            un_ref[0].astype(F32) * has_next], axis=0)


def _sc_out_kernel(x_ref, mod_ref, b_ref, u_ref, up_ref, un_ref, cw_ref, w_ref, o_ref):
    ext = _with_halo(u_ref, up_ref, un_ref)
    n = ext.shape[0]
    tm = n - 2 * HALO
    taps = cw_ref.shape[0]
    conv = None
    for k in range(taps):
        off = k - taps // 2
        sh = ext if off == 0 else pltpu.roll(ext, (-off) % n, axis=0)
        term = cw_ref[k:k + 1, :] * sh[HALO:HALO + tm]
        conv = term if conv is None else conv + term
    z = (b_ref[0].astype(F32) * conv).astype(BF16)
    o_ref[0] = x_ref[0] + mod_ref[0][2:3] * _dot(z, w_ref[...])


def _cf_out_kernel(taps, x_ref, mod_ref, u_ref, up_ref, un_ref, dw_ref, dwb_ref, lng_ref, lnb_ref,
                   w_ref, b2_ref, o_ref):
    ext = _with_halo(u_ref, up_ref, un_ref)
    n = ext.shape[0]
    tm = n - 2 * HALO
    base = HALO - taps // 2
    acc = None
    for s in range(SUBLANES):
        sh = ext if s == 0 else pltpu.roll(ext, n - s, axis=0)
        for p in range((base + taps - 1) // SUBLANES + 1):
            k = SUBLANES * p + s - base
            if 0 <= k < taps:
                term = dw_ref[k:k + 1, :] * sh[SUBLANES * p:SUBLANES * p + tm]
                acc = term if acc is None else acc + term
    u2 = acc + dwb_ref[...]
    mu = jnp.mean(u2, axis=-1, keepdims=True)
    cen = u2 - mu
    var = jnp.mean(cen * cen, axis=-1, keepdims=True)
    y = cen * lax.rsqrt(var + EPS) * lng_ref[...] + lnb_ref[...]
    y = _dot(_silu(y).astype(BF16), w_ref[...]) + b2_ref[...]
    o_ref[0] = x_ref[0] + mod_ref[0][2:3] * y


def _mixer_out(body, name, x, mod, toks, halo_of, consts, tm_pref=512):
    b, l, d = x.shape
    tm = _tile(l, tm_pref)
    per = tm // HALO
    last = l // HALO - 1
    tok = pl.BlockSpec((1, tm, d), lambda bi, i: (bi, i, 0))
    full = lambda a: pl.BlockSpec(a.shape, lambda bi, i: (0,) * a.ndim)
    in_specs = ([tok, pl.BlockSpec((1, MOD_ROWS, d), lambda bi, i: (bi, 0, 0))] + [tok] * len(toks)
                + [pl.BlockSpec((1, HALO, d), lambda bi, i: (bi, jnp.maximum(i * per - 1, 0), 0)),
                   pl.BlockSpec((1, HALO, d), lambda bi, i: (bi, jnp.minimum((i + 1) * per, last), 0))]
                + [full(c) for c in consts])
    return pl.pallas_call(
        body,
        grid=(b, l // tm),
        in_specs=in_specs,
        out_specs=tok,
        out_shape=jax.ShapeDtypeStruct((b, l, d), F32),
        compiler_params=_params("parallel", "parallel"),
        name=name,
    )(x, mod, *toks, halo_of, halo_of, *consts)


def _attn_out_kernel(x_ref, mod_ref, a_ref, w_ref, o_ref):
    a = jnp.concatenate([a_ref[0, p] for p in range(a_ref.shape[1])], axis=1)
    o_ref[0] = x_ref[0] + mod_ref[0][2:3] * _dot(a, w_ref[...])


def _attn_out(x, mod, a, w, tm_pref=512):
    b, l, d = x.shape
    tm = _tile(l, tm_pref)
    tok = pl.BlockSpec((1, tm, d), lambda bi, i: (bi, i, 0))
    return pl.pallas_call(
        _attn_out_kernel,
        grid=(b, l // tm),
        in_specs=[tok, pl.BlockSpec((1, MOD_ROWS, d), lambda bi, i: (bi, 0, 0)),
                  pl.BlockSpec((1, d // LANES, tm, LANES), lambda bi, i: (bi, 0, i, 0)),
                  pl.BlockSpec(w.shape, lambda bi, i: (0, 0))],
        out_specs=tok,
        out_shape=jax.ShapeDtypeStruct((b, l, d), F32),
        compiler_params=_params("parallel", "parallel"),
        name="attn_out",
    )(x, mod, a, w)


def _swiglu_step(hx, wg_ref, wu_ref, wd_ref, acc_s):
    g = _dot(hx, wg_ref[...])
    u = _dot(hx, wu_ref[...])
    acc_s[...] += _dot((_silu(g) * u).astype(BF16), wd_ref[...])


def _ffn_kernel(x_ref, mod_ref, g_ref, wg_ref, wu_ref, wd_ref, o_ref, hx_s, acc_s):
    f = pl.program_id(2)

    @pl.when(f == 0)
    def _():
        mod = mod_ref[0]
        hx_s[...] = _norm_mod(x_ref[0], g_ref[...], mod[3:4], mod[4:5]).astype(BF16)
        acc_s[...] = jnp.zeros_like(acc_s)

    _swiglu_step(hx_s[...], wg_ref, wu_ref, wd_ref, acc_s)

    @pl.when(f == pl.num_programs(2) - 1)
    def _():
        o_ref[0] = x_ref[0] + mod_ref[0][5:6] * acc_s[...]


def _ffn_chunk(fdim, pref):
    best = LANES
    for k in range(1, fdim // LANES + 1):
        t = k * LANES
        if fdim % t == 0 and t <= pref:
            best = t
    return best


def _ffn(x, mod, g, wg, wu, wd, tm_pref=512, tf_pref=1408):
    b, l, d = x.shape
    fdim = wg.shape[1]
    tm = _tile(l, tm_pref)
    tf = _ffn_chunk(fdim, tf_pref)
    tok = pl.BlockSpec((1, tm, d), lambda bi, i, f: (bi, i, 0))
    return pl.pallas_call(
        _ffn_kernel,
        grid=(b, l // tm, fdim // tf),
        in_specs=[tok, pl.BlockSpec((1, MOD_ROWS, d), lambda bi, i, f: (bi, 0, 0)),
                  pl.BlockSpec(g.shape, lambda bi, i, f: (0, 0)),
                  pl.BlockSpec((d, tf), lambda bi, i, f: (0, f)),
                  pl.BlockSpec((d, tf), lambda bi, i, f: (0, f)),
                  pl.BlockSpec((tf, d), lambda bi, i, f: (f, 0))],
        out_specs=tok,
        out_shape=jax.ShapeDtypeStruct((b, l, d), F32),
        scratch_shapes=[pltpu.VMEM((tm, d), BF16), pltpu.VMEM((tm, d), F32)],
        compiler_params=_params("parallel", "parallel", "arbitrary"),
        name="ffn",
    )(x, mod, g, wg, wu, wd)


def _to_token_tiled(ref, x):
    s = x.shape[1] // LANES
    for c, chunk in enumerate(_lane_chunks(x)):
        ref[pl.ds(c, x.shape[0], stride=s), :] = chunk


def _from_token_tiled(ref, tokens):
    s = ref.shape[0] // tokens
    return jnp.concatenate([ref[pl.ds(c, tokens, stride=s), :] for c in range(s)], axis=1)


def _router_kernel(n_exp, x_ref, mod_ref, g_ref, wr_ref, hx_ref, meta_ref, cnt_ref, carry_s):
    @pl.when((pl.program_id(0) == 0) & (pl.program_id(1) == 0))
    def _():
        carry_s[...] = jnp.zeros_like(carry_s)

    mod = mod_ref[0]
    hx = _norm_mod(x_ref[0], g_ref[...], mod[3:4], mod[4:5])
    _to_token_tiled(hx_ref, hx)
    tm = hx.shape[0]
    logits = jnp.dot(hx, wr_ref[...], precision=lax.Precision.HIGHEST, preferred_element_type=F32)
    lane = lax.broadcasted_iota(jnp.int32, (tm, LANES), 1).astype(F32)
    logits = jnp.where(lane < n_exp, logits, -jnp.inf)
    m1 = jnp.max(logits, axis=-1, keepdims=True)
    i1 = jnp.min(jnp.where(logits == m1, lane, float(LANES)), axis=-1, keepdims=True)
    rest = jnp.where(lane == i1, -jnp.inf, logits)
    m2 = jnp.max(rest, axis=-1, keepdims=True)
    i2 = jnp.min(jnp.where(rest == m2, lane, float(LANES)), axis=-1, keepdims=True)
    e = jnp.exp(m2 - m1)
    p1 = 1.0 / (1.0 + e)
    p2 = e / (1.0 + e)
    oh1 = lane == i1
    oh2 = lane == i2
    a = jnp.where(oh1 | oh2, 1.0, 0.0)
    below = jnp.where(lax.broadcasted_iota(jnp.int32, (tm, tm), 1)
                      < lax.broadcasted_iota(jnp.int32, (tm, tm), 0), 1.0, 0.0).astype(BF16)
    pre = _dot(below, a.astype(BF16)) + carry_s[0:1, :]
    r1 = jnp.sum(jnp.where(oh1, pre, 0.0), axis=-1, keepdims=True)
    r2 = jnp.sum(jnp.where(oh2, pre, 0.0), axis=-1, keepdims=True)
    carry_s[...] = carry_s[...] + jnp.sum(a, axis=0, keepdims=True)
    cnt_ref[...] = carry_s[...]
    meta = jnp.zeros((tm, LANES), F32)
    for col, val in enumerate((i1, i2, r1, r2, p1, p2)):
        meta = jnp.where(lane == col, val, meta)
    meta_ref[0] = meta


def _expert_kernel(te, s, nf, n_tok, te_ref, src0_ref, src_ref, dst_ref, dstl_ref, hx_hbm, wg_ref, wu_ref, wd_ref,
                   y_hbm, xbuf, ybuf, hx_s, acc_s, gsem, ssem):
    del te_ref
    r = pl.program_id(0)
    f = pl.program_id(1)
    rows = te * s
    per = te // nf
    cur = r % 2
    nxt = 1 - cur
    spare0 = 2 * n_tok

    def fetch(i, tok, slot):
        return pltpu.make_async_copy(hx_hbm.at[pl.ds(pl.multiple_of(tok * s, s), s)],
                                     xbuf.at[slot, pl.ds(pl.multiple_of(i * s, s), s)], gsem.at[slot])

    def send(i, row, slot):
        return pltpu.make_async_copy(ybuf.at[slot, pl.ds(pl.multiple_of(i * s, s), s)],
                                     y_hbm.at[pl.ds(pl.multiple_of(row * s, s), s)], ssem.at[slot])

    def all_fetched(slot):
        return pltpu.make_async_copy(hx_hbm.at[pl.ds(0, rows)], xbuf.at[slot], gsem.at[slot])

    def all_sent(slot):
        return pltpu.make_async_copy(ybuf.at[slot], y_hbm.at[pl.ds(0, rows)], ssem.at[slot])

    @pl.when((r == 0) & (f == 0))
    def _():
        ybuf[...] = jnp.zeros_like(ybuf)
        clear = pltpu.make_async_copy(ybuf.at[0], y_hbm.at[pl.ds(spare0 * s, rows)], ssem.at[0])
        clear.start()
        clear.wait()

        def first(i, carry):
            fetch(i, src0_ref[0, 0, i], 0).start()
            return carry

        lax.fori_loop(0, te, first, 0)

    @pl.when(f == 0)
    def _():
        all_fetched(cur).wait()
        hx_s[...] = _from_token_tiled(xbuf.at[cur], te).astype(BF16)
        acc_s[...] = jnp.zeros_like(acc_s)

    for j in range(per):
        i = f * per + j
        fetch(i, src_ref[0, 0, i], nxt).start()
        send(i, jnp.where(r > 0, dst_ref[0, 0, i], spare0 + nxt * te + i), nxt).start()

    _swiglu_step(hx_s[...], wg_ref.at[0], wu_ref.at[0], wd_ref.at[0], acc_s)

    @pl.when(f == nf - 1)
    def _():
        @pl.when(r > 0)
        def _():
            all_sent(cur).wait()

        _to_token_tiled(ybuf.at[cur], acc_s[...])

        @pl.when(r == pl.num_programs(0) - 1)
        def _():
            def final(i, carry):
                send(i, dstl_ref[0, 0, i], cur).start()
                return carry

            lax.fori_loop(0, te, final, 0)
            all_sent(nxt).wait()
            all_sent(cur).wait()
            all_fetched(nxt).wait()


def _combine_kernel(x_ref, meta_ref, mod_ref, y1_ref, y2_ref, o_ref):
    tc = x_ref.shape[0]
    meta = meta_ref[...]
    y = meta[:, 4:5] * _from_token_tiled(y1_ref, tc) + meta[:, 5:6] * _from_token_tiled(y2_ref, tc)
    o_ref[...] = x_ref[...] + mod_ref[0][5:6] * y


def _moe(x, mod, g, w_router, wg, wu, wd, tm_pref=512, te_pref=512, tf_pref=1792):
    b, l, d = x.shape
    n = b * l
    s = d // LANES
    n_exp, _, fdim = wg.shape
    tm = _tile(l, tm_pref)
    te = _tile(2 * n, min(te_pref, max(tm_pref // 2, n // n_exp)))
    tf = _ffn_chunk(fdim, tf_pref)
    nf = fdim // tf
    assert te % nf == 0 and n % tm == 0 and (2 * te) % tm == 0
    wr = jnp.zeros((d, LANES), F32).at[:, :n_exp].set(w_router)
    per_b = l // tm

    hx, meta, cnt = pl.pallas_call(
        functools.partial(_router_kernel, n_exp),
        grid=(b, per_b),
        in_specs=[pl.BlockSpec((1, tm, d), lambda bi, i: (bi, i, 0)),
                  pl.BlockSpec((1, MOD_ROWS, d), lambda bi, i: (bi, 0, 0)),
                  pl.BlockSpec(g.shape, lambda bi, i: (0, 0)),
                  pl.BlockSpec(wr.shape, lambda bi, i: (0, 0))],
        out_specs=[pl.BlockSpec((tm * s, LANES), lambda bi, i: (bi * per_b + i, 0)),
                   pl.BlockSpec((1, tm, LANES), lambda bi, i: (bi, i, 0)),
                   pl.BlockSpec((SUBLANES, LANES), lambda bi, i: (0, 0))],
        out_shape=[jax.ShapeDtypeStruct((n * s, LANES), F32), jax.ShapeDtypeStruct((b, l, LANES), F32),
                   jax.ShapeDtypeStruct((SUBLANES, LANES), F32)],
        scratch_shapes=[pltpu.VMEM((SUBLANES, LANES), F32)],
        compiler_params=_params("arbitrary", "arbitrary"),
        name="moe_router",
    )(x, mod, g, wr)

    meta = meta.reshape(n, LANES)
    ids = meta[:, 0:2].astype(jnp.int32)
    rank = meta[:, 2:4].astype(jnp.int32)
    counts = cnt[0, :n_exp].astype(jnp.int32)
    tiles = (counts + te - 1) // te
    tile_end = jnp.cumsum(tiles)
    row_start = (tile_end - tiles) * te
    start_of = jnp.sum(jnp.where(ids[:, :, None] == jnp.arange(n_exp, dtype=jnp.int32), row_start, 0), axis=-1)
    pos = (start_of + rank).reshape(-1)
    n_tiles = (2 * n) // te + n_exp
    r = jnp.arange(n_tiles, dtype=jnp.int32)
    tile_e = jnp.minimum(jnp.sum((r[:, None] >= tile_end[None, :]).astype(jnp.int32), axis=1), n_exp - 1)
    p = jnp.arange(n_tiles * te, dtype=jnp.int32)
    spare = 2 * n + ((p // te) % 2) * te + p % te
    a = jnp.arange(2 * n, dtype=jnp.int32)
    dst = spare.at[pos].set((a % 2) * n + a // 2, unique_indices=True)
    src = jnp.where(dst < 2 * n, dst % n, 0).reshape(n_tiles, 1, te)
    dst = dst.reshape(n_tiles, 1, te)

    smem = lambda fn: pl.BlockSpec((1, 1, te), fn, memory_space=pltpu.SMEM)
    y = pl.pallas_call(
        functools.partial(_expert_kernel, te, s, nf, n),
        grid_spec=pltpu.PrefetchScalarGridSpec(
            num_scalar_prefetch=1,
            grid=(n_tiles, nf),
            in_specs=[smem(lambda r, f, e: (0, 0, 0)),
                      smem(lambda r, f, e: (jnp.minimum(r + 1, n_tiles - 1), 0, 0)),
                      smem(lambda r, f, e: (jnp.maximum(r - 1, 0), 0, 0)),
                      smem(lambda r, f, e: (n_tiles - 1, 0, 0)),
                      pl.BlockSpec(memory_space=pl.ANY),
                      pl.BlockSpec((1, d, tf), lambda r, f, e: (e[r], 0, f)),
                      pl.BlockSpec((1, d, tf), lambda r, f, e: (e[r], 0, f)),
                      pl.BlockSpec((1, tf, d), lambda r, f, e: (e[r], f, 0))],
            out_specs=pl.BlockSpec(memory_space=pl.ANY),
            scratch_shapes=[pltpu.VMEM((2, te * s, LANES), F32), pltpu.VMEM((2, te * s, LANES), F32),
                            pltpu.VMEM((te, d), BF16), pltpu.VMEM((te, d), F32),
                            pltpu.SemaphoreType.DMA((2,)), pltpu.SemaphoreType.DMA((2,))]),
        out_shape=jax.ShapeDtypeStruct(((2 * n + 2 * te) * s, LANES), F32),
        compiler_params=_params("arbitrary", "arbitrary"),
        name="moe_experts",
    )(tile_e, src, src, dst, dst, hx, wg, wu, wd)

    flat = pl.BlockSpec((tm, d), lambda i: (i, 0))
    out = pl.pallas_call(
        _combine_kernel,
        grid=(n // tm,),
        in_specs=[flat, pl.BlockSpec((tm, LANES), lambda i: (i, 0)),
                  pl.BlockSpec((1, MOD_ROWS, d), lambda i: (i // per_b, 0, 0)),
                  pl.BlockSpec((tm * s, LANES), lambda i: (i, 0)),
                  pl.BlockSpec((tm * s, LANES), lambda i: (i + n // tm, 0))],
        out_specs=flat,
        out_shape=jax.ShapeDtypeStruct((n, d), F32),
        compiler_params=_params("parallel"),
        name="moe_combine",
    )(x.reshape(n, d), meta, mod, y, y)
    return out.reshape(b, l, d)


def _attn_kernel(q_ref, kp_ref, kc_ref, kn_ref, vp_ref, vc_ref, vn_ref, kx_ref, vx_ref, tab_ref, vm_ref, o_ref):
    hd = LANES // 2
    n_pairs, tq = q_ref.shape[1], q_ref.shape[2]
    lane = lax.broadcasted_iota(jnp.int32, (tq, LANES), 1)
    n_key_tiles = 3 * tq // LANES
    nt = (((1,), (1,)), ((), ()))

    def pair(p, carry):
        q2 = q_ref[0, p]
        k2 = jnp.concatenate([kp_ref[0, p], kc_ref[0, p], kn_ref[0, p]], axis=0)
        v2 = jnp.concatenate([vp_ref[0, p], vc_ref[0, p], vn_ref[0, p]], axis=0)
        kx2 = kx_ref[0, p]
        vx2 = vx_ref[0, p]
        halves = []
        for hh in range(2):
            qm = jnp.where((lane >= hh * hd) & (lane < (hh + 1) * hd), q2, jnp.zeros_like(q2))
            s_loc = lax.dot_general(qm, k2, nt, preferred_element_type=F32)
            s_ctx = lax.dot_general(qm, kx2, nt, preferred_element_type=F32)
            h = 2 * p + hh
            bias = jnp.concatenate(
                [jnp.concatenate([tab_ref[h, WIN_R // 2 - 1 + 2 * j - qr]
                                  + vm_ref[0, qr * n_key_tiles + j:qr * n_key_tiles + j + 1, :]
                                  for j in range(n_key_tiles)], axis=1)
                 for qr in range(Q_ROWS)], axis=0)
            s_loc = s_loc + bias
            m = jnp.maximum(jnp.max(s_loc, axis=-1, keepdims=True), jnp.max(s_ctx, axis=-1, keepdims=True))
            p_loc = jnp.exp(s_loc - m)
            p_ctx = jnp.exp(s_ctx - m)
            den = jnp.sum(p_loc, axis=-1, keepdims=True) + jnp.sum(p_ctx, axis=-1, keepdims=True)
            halves.append((_dot(p_loc.astype(BF16), v2) + _dot(p_ctx.astype(BF16), vx2)) / den)
        o_ref[0, p] = jnp.where(lane < hd, halves[0], halves[1]).astype(BF16)
        return carry

    lax.fori_loop(0, n_pairs, pair, 0)


def _attn_tables(rpb, rows):
    w = GRID_W
    qc = np.arange(w)
    cs = np.clip(qc - WIN_C // 2, 0, w - WIN_C)
    kc = np.arange(w)
    in_win = (kc[None, :] >= cs[:, None]) & (kc[None, :] < cs[:, None] + WIN_C)
    rel_c = np.clip(kc[None, :] - qc[:, None] + WIN_C - 1, 0, 2 * WIN_C - 2)
    n_m = 2 * WIN_R - 2
    rel_r = np.arange(n_m)[:, None] + np.arange(2)[None, :]
    g = rpb[:, rel_r][:, :, :, rel_c]
    g = jnp.where(in_win[None, None, None], g, NEG)
    tab = jnp.transpose(g, (0, 1, 3, 2, 4)).reshape(rpb.shape[0], n_m, w, 2 * w).astype(F32)

    key_tiles = 3 * Q_ROWS // 2
    nblk = rows // Q_ROWS
    vm = np.full((3, Q_ROWS * key_tiles, 2 * w), NEG, np.float32)
    for case, blk in enumerate((0, 1, nblk - 1)):
        r0 = blk * Q_ROWS
        for qr in range(Q_ROWS):
            rs = min(max(r0 + qr - WIN_R // 2, 0), rows - WIN_R)
            for j in range(key_tiles):
                for dr in range(2):
                    krow = r0 - Q_ROWS + 2 * j + dr
                    if rs <= krow < rs + WIN_R:
                        vm[case, qr * key_tiles + j, dr * w:(dr + 1) * w] = 0.0
    return tab, jnp.asarray(vm)


def _attention(q, k, v, kx, vx, rpb):
    b, pairs, l, _ = q.shape
    rows = l // GRID_W
    assert GRID_W * 2 == LANES and Q_ROWS == WIN_R // 2 and rows % Q_ROWS == 0 and rows // Q_ROWS >= 3
    tq = Q_ROWS * GRID_W
    nblk = rows // Q_ROWS
    ctx = kx.shape[2]
    tab, vm = _attn_tables(rpb, rows)
    blk = lambda fn: pl.BlockSpec((1, pairs, tq, LANES), fn)
    prev = blk(lambda bi, i: (bi, 0, jnp.maximum(i - 1, 0), 0))
    cur = blk(lambda bi, i: (bi, 0, i, 0))
    nxt = blk(lambda bi, i: (bi, 0, jnp.minimum(i + 1, nblk - 1), 0))
    cx = pl.BlockSpec((1, pairs, ctx, LANES), lambda bi, i: (bi, 0, 0, 0))
    case = lambda bi, i: (jnp.where(i > 0, 1, 0) + jnp.where(i == nblk - 1, 1, 0), 0, 0)
    return pl.pallas_call(
        _attn_kernel,
        grid=(b, nblk),
        in_specs=[cur, prev, cur, nxt, prev, cur, nxt, cx, cx,
                  pl.BlockSpec(tab.shape, lambda bi, i: (0, 0, 0, 0)),
                  pl.BlockSpec((1,) + vm.shape[1:], case)],
        out_specs=cur,
        out_shape=jax.ShapeDtypeStruct((b, pairs, l, LANES), BF16),
        compiler_params=_params("parallel", "parallel"),
        name="attention",
    )(q, k, k, k, v, v, v, kx, vx, tab, vm)


def _short_conv(x, mod, g, w_in, conv_w, w_out):
    b_gate, u = _mixer_in("sc", x, mod, g, w_in.astype(BF16), (), 2)
    return _mixer_out(_sc_out_kernel, "short_conv_out", x, mod, (b_gate, u), u, (conv_w, w_out.astype(BF16)))


def _conformer(x, mod, g, w1, b1, dw, dw_b, ln_g, ln_b, w2, b2):
    assert dw.shape[0] // 2 <= HALO
    row = lambda a: a.reshape(1, -1)
    (u,) = _mixer_in("cf", x, mod, g, w1.astype(BF16), (row(b1),), 1)
    body = functools.partial(_cf_out_kernel, dw.shape[0])
    return _mixer_out(body, "conformer_out", x, mod, (u,), u,
                      (dw, row(dw_b), row(ln_g), row(ln_b), w2.astype(BF16), row(b2)), tm_pref=256)


def _qkv(x, mod, g, w_qkv, q_g, k_g):
    d = x.shape[-1]
    hd = q_g.shape[0]
    heads = d // hd
    w = min(MXU_DIM, d)
    bd = jnp.asarray(np.kron(np.eye(w // hd), np.ones((hd, hd))), BF16)
    qg = (jnp.tile(q_g, heads) * hd ** -0.5).reshape(1, d)
    kg = jnp.tile(k_g, heads).reshape(1, d)
    return _mixer_in("qkv", x, mod, g, w_qkv.astype(BF16), (qg, kg, bd), 3, hd=hd)


def kernel(x, c, ctx, c_ctx, ada_w, ada_b, norm_mix_g, norm_ffn_g, sc_w_in, sc_conv_w, sc_w_out, cf_w1, cf_b1, cf_dw, cf_dw_b, cf_ln_g, cf_ln_b, cf_w2, cf_b2, na_w_qkv, na_q_g, na_k_g, na_rpb, na_w_out, ffn_w_gate, ffn_w_up, ffn_w_down, moe_router, moe_w_gate, moe_w_up, moe_w_down):
    b, l, d = x.shape
    depth = ada_w.shape[0]
    rb = -(-(b + 1) // SUBLANES) * SUBLANES
    cvec = jnp.zeros((rb, d), F32).at[:b].set(c).at[b].set(c_ctx)
    ada = _ada(cvec, ada_w, ada_b)
    pad = ((0, 0), (0, MOD_ROWS - N_MOD), (0, 0))

    readers = [i for i in range(depth) if i % N_MIXERS == 2]
    last_reader = readers[-1] if readers else -1
    for i in range(depth):
        m, j, f = i % N_MIXERS, i // N_MIXERS, i // 2
        upd = i < last_reader
        mx = jnp.pad(ada[i, :b].reshape(b, N_MOD, d), pad)
        mc = jnp.broadcast_to(jnp.pad(ada[i, b].reshape(1, N_MOD, d), pad), (b, MOD_ROWS, d))
        gm = norm_mix_g[i].reshape(1, d)
        gf = norm_ffn_g[i].reshape(1, d)
        streams = [(x, mx)] + ([(ctx, mc)] if upd else [])
        if m == 0:
            mixed = [_short_conv(s, md, gm, sc_w_in[j], sc_conv_w[j], sc_w_out[j]) for s, md in streams]
        elif m == 1:
            mixed = [_conformer(s, md, gm, cf_w1[j], cf_b1[j], cf_dw[j], cf_dw_b[j], cf_ln_g[j], cf_ln_b[j],
                                cf_w2[j], cf_b2[j]) for s, md in streams]
        else:
            assert not upd, "a context update after an attention layer is not needed for this depth"
            q, k, v = _qkv(x, mx, gm, na_w_qkv[j], na_q_g[j], na_k_g[j])
            _, kc, vc = _qkv(ctx, mc, gm, na_w_qkv[j], na_q_g[j], na_k_g[j])
            mixed = [_attn_out(x, mx, _attention(q, k, v, kc, vc, na_rpb[j]), na_w_out[j].astype(BF16))]
        outs = []
        for s, md in zip(mixed, (mx, mc)):
            if i % 2 == 0:
                outs.append(_ffn(s, md, gf, ffn_w_gate[f].astype(BF16), ffn_w_up[f].astype(BF16),
                                 ffn_w_down[f].astype(BF16)))
            else:
                outs.append(_moe(s, md, gf, moe_router[f], moe_w_gate[f].astype(BF16),
                                 moe_w_up[f].astype(BF16), moe_w_down[f].astype(BF16)))
        x = outs[0]
        if upd:
            ctx = outs[1]
    return x
```

```python
import functools

import jax
import jax.numpy as jnp
import numpy as np
from jax import lax
from jax.experimental import pallas as pl
from jax.experimental.pallas import tpu as pltpu

GRID_W = 64
WIN_R = 8
WIN_C = 16
N_MIXERS = 3
N_MOD = 6
EPS = 1e-6

LANES = 128
SUBLANES = 8
MXU_DIM = 256
VMEM_LIMIT_BYTES = 56 * 1024 * 1024

HALO = 16
Q_ROWS = 4
NEG = -1e30
MOD_ROWS = 8

F32 = jnp.float32
BF16 = jnp.bfloat16


def _params(*sem):
    return pltpu.CompilerParams(dimension_semantics=sem, vmem_limit_bytes=VMEM_LIMIT_BYTES)


def _tile(n, pref):
    t = min(n, pref)
    while n % t or (t % HALO and t != n):
        t -= 1
    return t


def _dot(a, b):
    return jnp.dot(a, b, preferred_element_type=F32)


def _norm_mod(x, g, shift, scale):
    ms = jnp.mean(x * x, axis=-1, keepdims=True)
    return (x * lax.rsqrt(ms + EPS) * g) * (1.0 + scale) + shift


def _silu(x):
    return x * jax.nn.sigmoid(x)


def _lane_chunks(x):
    return [x[:, c * LANES:(c + 1) * LANES] for c in range(x.shape[1] // LANES)]


def _ada_kernel(c_ref, w_ref, b_ref, o_ref):
    o_ref[0] = jnp.dot(_silu(c_ref[...]), w_ref[0], precision=lax.Precision.HIGHEST,
                       preferred_element_type=F32) + b_ref[0]


def _ada(cvec, ada_w, ada_b):
    depth, d, nd = ada_w.shape
    rb = cvec.shape[0]
    tn = _tile(nd, 1024)
    return pl.pallas_call(
        _ada_kernel,
        grid=(depth, nd // tn),
        in_specs=[pl.BlockSpec((rb, d), lambda i, j: (0, 0)),
                  pl.BlockSpec((1, d, tn), lambda i, j: (i, 0, j)),
                  pl.BlockSpec((1, 1, tn), lambda i, j: (i, 0, j))],
        out_specs=pl.BlockSpec((1, rb, tn), lambda i, j: (i, 0, j)),
        out_shape=jax.ShapeDtypeStruct((depth, rb, nd), F32),
        compiler_params=_params("arbitrary", "arbitrary"),
        name="ada",
    )(cvec, ada_w, ada_b.reshape(depth, 1, nd))


def _head_norm(t, gain, bd_ref, hd):
    tt = t * t
    hi = tt.astype(BF16)
    lo = (tt - hi.astype(F32)).astype(BF16)
    bd = bd_ref[...]
    w = bd.shape[0]
    parts = []
    for c in range(t.shape[1] // w):
        sl = slice(c * w, (c + 1) * w)
        parts.append(_dot(hi[:, sl], bd) + _dot(lo[:, sl], bd))
    ss = jnp.concatenate(parts, axis=1) if len(parts) > 1 else parts[0]
    return t * lax.rsqrt(ss * (1.0 / hd) + EPS) * gain


def _in_kernel(mode, tn, hd, x_ref, mod_ref, g_ref, w_ref, *rest):
    mod = mod_ref[0]
    hx = _norm_mod(x_ref[0], g_ref[...], mod[0:1], mod[1:2]).astype(BF16)
    d = hx.shape[1]

    def mm(part, j):
        lo = part * d + j * tn
        return _dot(hx, w_ref[:, lo:lo + tn])

    for j in range(d // tn):
        sl = slice(j * tn, (j + 1) * tn)
        if mode == "sc":
            b_out, u_out = rest
            b_out[0, :, sl] = mm(0, j).astype(BF16)
            u_out[0, :, sl] = (mm(1, j) * mm(2, j)).astype(BF16)
        elif mode == "cf":
            bias_ref, u_out = rest
            a = mm(0, j) + bias_ref[:, sl]
            gte = mm(1, j) + bias_ref[:, d + j * tn:d + (j + 1) * tn]
            u_out[0, :, sl] = (a * jax.nn.sigmoid(gte)).astype(BF16)
        else:
            qg_ref, kg_ref, bd_ref, q_out, k_out, v_out = rest
            q = _head_norm(mm(0, j), qg_ref[:, sl], bd_ref, hd).astype(BF16)
            k = _head_norm(mm(1, j), kg_ref[:, sl], bd_ref, hd).astype(BF16)
            v = mm(2, j).astype(BF16)
            for c in range(tn // LANES):
                pair = j * (tn // LANES) + c
                q_out[0, pair] = q[:, c * LANES:(c + 1) * LANES]
                k_out[0, pair] = k[:, c * LANES:(c + 1) * LANES]
                v_out[0, pair] = v[:, c * LANES:(c + 1) * LANES]


def _mixer_in(mode, x, mod, g, w, extras, n_out, hd=1, tm_pref=512):
    b, l, d = x.shape
    tm = _tile(l, tm_pref)
    tn = _tile(d, 512)
    full = lambda a: pl.BlockSpec(a.shape, lambda bi, i: (0,) * a.ndim)
    tok = pl.BlockSpec((1, tm, d), lambda bi, i: (bi, i, 0))
    if mode == "qkv":
        out_spec = pl.BlockSpec((1, d // LANES, tm, LANES), lambda bi, i: (bi, 0, i, 0))
        out_shape = jax.ShapeDtypeStruct((b, d // LANES, l, LANES), BF16)
    else:
        out_spec, out_shape = tok, jax.ShapeDtypeStruct((b, l, d), BF16)
    return pl.pallas_call(
        functools.partial(_in_kernel, mode, tn, hd),
        grid=(b, l // tm),
        in_specs=[tok, pl.BlockSpec((1, MOD_ROWS, d), lambda bi, i: (bi, 0, 0)), full(g), full(w)]
                 + [full(e) for e in extras],
        out_specs=[out_spec] * n_out,
        out_shape=[out_shape] * n_out,
        compiler_params=_params("parallel", "parallel"),
        name="mixer_in_" + mode,
    )(x, mod, g, w, *extras)


def _with_halo(u_ref, up_ref, un_ref):
    i = pl.program_id(1)
    has_prev = jnp.where(i > 0, 1.0, 0.0)
    has_next = jnp.where(i < pl.num_programs(1) - 1, 1.0, 0.0)
    return jnp.concatenate([up_ref[0].astype(F32) * has_prev, u_ref[0].astype(F32),
                            un_ref[0].astype(F32) * has_next], axis=0)


def _sc_out_kernel(x_ref, mod_ref, b_ref, u_ref, up_ref, un_ref, cw_ref, w_ref, o_ref):
    ext = _with_halo(u_ref, up_ref, un_ref)
    n = ext.shape[0]
    tm = n - 2 * HALO
    taps = cw_ref.shape[0]
    conv = None
    for k in range(taps):
        off = k - taps // 2
        sh = ext if off == 0 else pltpu.roll(ext, (-off) % n, axis=0)
        term = cw_ref[k:k + 1, :] * sh[HALO:HALO + tm]
        conv = term if conv is None else conv + term
    z = (b_ref[0].astype(F32) * conv).astype(BF16)
    o_ref[0] = x_ref[0] + mod_ref[0][2:3] * _dot(z, w_ref[...])


def _cf_out_kernel(taps, x_ref, mod_ref, u_ref, up_ref, un_ref, dw_ref, dwb_ref, lng_ref, lnb_ref,
                   w_ref, b2_ref, o_ref):
    ext = _with_halo(u_ref, up_ref, un_ref)
    n = ext.shape[0]
    tm = n - 2 * HALO
    base = HALO - taps // 2
    acc = None
    for s in range(SUBLANES):
        sh = ext if s == 0 else pltpu.roll(ext, n - s, axis=0)
        for p in range((base + taps - 1) // SUBLANES + 1):
            k = SUBLANES * p + s - base
            if 0 <= k < taps:
                term = dw_ref[k:k + 1, :] * sh[SUBLANES * p:SUBLANES * p + tm]
                acc = term if acc is None else acc + term
    u2 = acc + dwb_ref[...]
    mu = jnp.mean(u2, axis=-1, keepdims=True)
    cen = u2 - mu
    var = jnp.mean(cen * cen, axis=-1, keepdims=True)
    y = cen * lax.rsqrt(var + EPS) * lng_ref[...] + lnb_ref[...]
    y = _dot(_silu(y).astype(BF16), w_ref[...]) + b2_ref[...]
    o_ref[0] = x_ref[0] + mod_ref[0][2:3] * y


def _mixer_out(body, name, x, mod, toks, halo_of, consts, tm_pref=512):
    b, l, d = x.shape
    tm = _tile(l, tm_pref)
    per = tm // HALO
    last = l // HALO - 1
    tok = pl.BlockSpec((1, tm, d), lambda bi, i: (bi, i, 0))
    full = lambda a: pl.BlockSpec(a.shape, lambda bi, i: (0,) * a.ndim)
    in_specs = ([tok, pl.BlockSpec((1, MOD_ROWS, d), lambda bi, i: (bi, 0, 0))] + [tok] * len(toks)
                + [pl.BlockSpec((1, HALO, d), lambda bi, i: (bi, jnp.maximum(i * per - 1, 0), 0)),
                   pl.BlockSpec((1, HALO, d), lambda bi, i: (bi, jnp.minimum((i + 1) * per, last), 0))]
                + [full(c) for c in consts])
    return pl.pallas_call(
        body,
        grid=(b, l // tm),
        in_specs=in_specs,
        out_specs=tok,
        out_shape=jax.ShapeDtypeStruct((b, l, d), F32),
        compiler_params=_params("parallel", "parallel"),
        name=name,
    )(x, mod, *toks, halo_of, halo_of, *consts)


def _attn_out_kernel(x_ref, mod_ref, a_ref, w_ref, o_ref):
    a = jnp.concatenate([a_ref[0, p] for p in range(a_ref.shape[1])], axis=1)
    o_ref[0] = x_ref[0] + mod_ref[0][2:3] * _dot(a, w_ref[...])


def _attn_out(x, mod, a, w, tm_pref=512):
    b, l, d = x.shape
    tm = _tile(l, tm_pref)
    tok = pl.BlockSpec((1, tm, d), lambda bi, i: (bi, i, 0))
    return pl.pallas_call(
        _attn_out_kernel,
        grid=(b, l // tm),
        in_specs=[tok, pl.BlockSpec((1, MOD_ROWS, d), lambda bi, i: (bi, 0, 0)),
                  pl.BlockSpec((1, d // LANES, tm, LANES), lambda bi, i: (bi, 0, i, 0)),
                  pl.BlockSpec(w.shape, lambda bi, i: (0, 0))],
        out_specs=tok,
        out_shape=jax.ShapeDtypeStruct((b, l, d), F32),
        compiler_params=_params("parallel", "parallel"),
        name="attn_out",
    )(x, mod, a, w)


def _swiglu_step(hx, wg_ref, wu_ref, wd_ref, acc_s):
    tf = wg_ref.shape[1]
    sub = MXU_DIM if tf % MXU_DIM == 0 else tf
    acc = acc_s[...]
    for c in range(tf // sub):
        sl = slice(c * sub, (c + 1) * sub)
        g = _dot(hx, wg_ref[:, sl])
        u = _dot(hx, wu_ref[:, sl])
        acc = acc + _dot((_silu(g) * u).astype(BF16), wd_ref[sl, :])
    acc_s[...] = acc


def _ffn_kernel(x_ref, mod_ref, g_ref, wg_ref, wu_ref, wd_ref, o_ref, hx_s, acc_s):
    f = pl.program_id(2)

    @pl.when(f == 0)
    def _():
        mod = mod_ref[0]
        hx_s[...] = _norm_mod(x_ref[0], g_ref[...], mod[3:4], mod[4:5]).astype(BF16)
        acc_s[...] = jnp.zeros_like(acc_s)

    _swiglu_step(hx_s[...], wg_ref, wu_ref, wd_ref, acc_s)

    @pl.when(f == pl.num_programs(2) - 1)
    def _():
        o_ref[0] = x_ref[0] + mod_ref[0][5:6] * acc_s[...]


def _ffn_chunk(fdim, pref):
    best = LANES
    for k in range(1, fdim // LANES + 1):
        t = k * LANES
        if fdim % t == 0 and t <= pref:
            best = t
    return best


def _ffn(x, mod, g, wg, wu, wd, tm_pref=512, tf_pref=2816):
    b, l, d = x.shape
    fdim = wg.shape[1]
    tm = _tile(l, tm_pref)
    tf = _ffn_chunk(fdim, tf_pref)
    tok = pl.BlockSpec((1, tm, d), lambda bi, i, f: (bi, i, 0))
    return pl.pallas_call(
        _ffn_kernel,
        grid=(b, l // tm, fdim // tf),
        in_specs=[tok, pl.BlockSpec((1, MOD_ROWS, d), lambda bi, i, f: (bi, 0, 0)),
                  pl.BlockSpec(g.shape, lambda bi, i, f: (0, 0)),
                  pl.BlockSpec((d, tf), lambda bi, i, f: (0, f)),
                  pl.BlockSpec((d, tf), lambda bi, i, f: (0, f)),
                  pl.BlockSpec((tf, d), lambda bi, i, f: (f, 0))],
        out_specs=tok,
        out_shape=jax.ShapeDtypeStruct((b, l, d), F32),
        scratch_shapes=[pltpu.VMEM((tm, d), BF16), pltpu.VMEM((tm, d), F32)],
        compiler_params=_params("parallel", "parallel", "arbitrary"),
        name="ffn",
    )(x, mod, g, wg, wu, wd)


def _to_token_tiled(ref, x):
    s = x.shape[1] // LANES
    for c, chunk in enumerate(_lane_chunks(x)):
        ref[pl.ds(c, x.shape[0], stride=s), :] = chunk


def _from_token_tiled(ref, tokens):
    s = ref.shape[0] // tokens
    return jnp.concatenate([ref[pl.ds(c, tokens, stride=s), :] for c in range(s)], axis=1)


def _router_kernel(n_exp, x_ref, mod_ref, g_ref, wr_ref, hx_ref, meta_ref, cnt_ref, carry_s):
    @pl.when((pl.program_id(0) == 0) & (pl.program_id(1) == 0))
    def _():
        carry_s[...] = jnp.zeros_like(carry_s)

    mod = mod_ref[0]
    hx = _norm_mod(x_ref[0], g_ref[...], mod[3:4], mod[4:5])
    _to_token_tiled(hx_ref, hx)
    tm = hx.shape[0]
    logits = jnp.dot(hx, wr_ref[...], precision=lax.Precision.HIGHEST, preferred_element_type=F32)
    lane = lax.broadcasted_iota(jnp.int32, (tm, LANES), 1).astype(F32)
    logits = jnp.where(lane < n_exp, logits, -jnp.inf)
    m1 = jnp.max(logits, axis=-1, keepdims=True)
    i1 = jnp.min(jnp.where(logits == m1, lane, float(LANES)), axis=-1, keepdims=True)
    rest = jnp.where(lane == i1, -jnp.inf, logits)
    m2 = jnp.max(rest, axis=-1, keepdims=True)
    i2 = jnp.min(jnp.where(rest == m2, lane, float(LANES)), axis=-1, keepdims=True)
    e = jnp.exp(m2 - m1)
    p1 = 1.0 / (1.0 + e)
    p2 = e / (1.0 + e)
    oh1 = lane == i1
    oh2 = lane == i2
    a = jnp.where(oh1 | oh2, 1.0, 0.0)
    below = jnp.where(lax.broadcasted_iota(jnp.int32, (tm, tm), 1)
                      < lax.broadcasted_iota(jnp.int32, (tm, tm), 0), 1.0, 0.0).astype(BF16)
    pre = _dot(below, a.astype(BF16)) + carry_s[0:1, :]
    r1 = jnp.sum(jnp.where(oh1, pre, 0.0), axis=-1, keepdims=True)
    r2 = jnp.sum(jnp.where(oh2, pre, 0.0), axis=-1, keepdims=True)
    carry_s[...] = carry_s[...] + jnp.sum(a, axis=0, keepdims=True)
    cnt_ref[...] = carry_s[...]
    meta = jnp.zeros((tm, LANES), F32)
    for col, val in enumerate((i1, i2, r1, r2, p1, p2)):
        meta = jnp.where(lane == col, val, meta)
    meta_ref[0] = meta


def _expert_kernel(te, s, nf, n_tok, te_ref, src0_ref, src_ref, dst_ref, dstl_ref, hx_hbm, wg_ref, wu_ref, wd_ref,
                   y_hbm, xbuf, ybuf, hx_s, acc_s, gsem, ssem):
    del te_ref
    r = pl.program_id(0)
    f = pl.program_id(1)
    rows = te * s
    per = te // nf
    cur = r % 2
    nxt = 1 - cur
    spare0 = 2 * n_tok

    def fetch(i, tok, slot):
        return pltpu.make_async_copy(hx_hbm.at[pl.ds(pl.multiple_of(tok * s, s), s)],
                                     xbuf.at[slot, pl.ds(pl.multiple_of(i * s, s), s)], gsem.at[slot])

    def send(i, row, slot):
        return pltpu.make_async_copy(ybuf.at[slot, pl.ds(pl.multiple_of(i * s, s), s)],
                                     y_hbm.at[pl.ds(pl.multiple_of(row * s, s), s)], ssem.at[slot])

    def all_fetched(slot):
        return pltpu.make_async_copy(hx_hbm.at[pl.ds(0, rows)], xbuf.at[slot], gsem.at[slot])

    def all_sent(slot):
        return pltpu.make_async_copy(ybuf.at[slot], y_hbm.at[pl.ds(0, rows)], ssem.at[slot])

    @pl.when((r == 0) & (f == 0))
    def _():
        ybuf[...] = jnp.zeros_like(ybuf)
        clear = pltpu.make_async_copy(ybuf.at[0], y_hbm.at[pl.ds(spare0 * s, rows)], ssem.at[0])
        clear.start()
        clear.wait()

        def first(i, carry):
            fetch(i, src0_ref[0, 0, i], 0).start()
            return carry

        lax.fori_loop(0, te, first, 0)

    @pl.when(f == 0)
    def _():
        all_fetched(cur).wait()
        hx_s[...] = _from_token_tiled(xbuf.at[cur], te).astype(BF16)
        acc_s[...] = jnp.zeros_like(acc_s)

    for j in range(per):
        i = f * per + j
        fetch(i, src_ref[0, 0, i], nxt).start()
        send(i, dst_ref[0, 0, i], nxt).start()

    _swiglu_step(hx_s[...], wg_ref.at[0, 0], wu_ref.at[0, 0], wd_ref.at[0, 0], acc_s)

    @pl.when(f == nf - 1)
    def _():
        @pl.when(r > 0)
        def _():
            all_sent(cur).wait()

        _to_token_tiled(ybuf.at[cur], acc_s[...])

        @pl.when(r == pl.num_programs(0) - 1)
        def _():
            def final(i, carry):
                send(i, dstl_ref[0, 0, i], cur).start()
                return carry

            lax.fori_loop(0, te, final, 0)
            all_sent(nxt).wait()
            all_sent(cur).wait()
            all_fetched(nxt).wait()


def _combine_kernel(x_ref, meta_ref, mod_ref, y1_ref, y2_ref, o_ref):
    tc = x_ref.shape[0]
    meta = meta_ref[...]
    y = meta[:, 4:5] * _from_token_tiled(y1_ref, tc) + meta[:, 5:6] * _from_token_tiled(y2_ref, tc)
    o_ref[...] = x_ref[...] + mod_ref[0][5:6] * y


def _moe(x, mod, g, w_router, wg, wu, wd, layer, tm_pref=512, te_pref=512, tf_pref=1792):
    b, l, d = x.shape
    n = b * l
    s = d // LANES
    n_exp, _, fdim = wg.shape[1:]
    tm = _tile(l, tm_pref)
    te = _tile(2 * n, min(te_pref, max(tm_pref // 2, n // n_exp)))
    tf = _ffn_chunk(fdim, tf_pref)
    nf = fdim // tf
    assert te % nf == 0 and n % tm == 0 and (2 * te) % tm == 0
    wr = jnp.zeros((d, LANES), F32).at[:, :n_exp].set(w_router)
    per_b = l // tm

    hx, meta, cnt = pl.pallas_call(
        functools.partial(_router_kernel, n_exp),
        grid=(b, per_b),
        in_specs=[pl.BlockSpec((1, tm, d), lambda bi, i: (bi, i, 0)),
                  pl.BlockSpec((1, MOD_ROWS, d), lambda bi, i: (bi, 0, 0)),
                  pl.BlockSpec(g.shape, lambda bi, i: (0, 0)),
                  pl.BlockSpec(wr.shape, lambda bi, i: (0, 0))],
        out_specs=[pl.BlockSpec((tm * s, LANES), lambda bi, i: (bi * per_b + i, 0)),
                   pl.BlockSpec((1, tm, LANES), lambda bi, i: (bi, i, 0)),
                   pl.BlockSpec((SUBLANES, LANES), lambda bi, i: (0, 0))],
        out_shape=[jax.ShapeDtypeStruct((n * s, LANES), F32), jax.ShapeDtypeStruct((b, l, LANES), F32),
                   jax.ShapeDtypeStruct((SUBLANES, LANES), F32)],
        scratch_shapes=[pltpu.VMEM((SUBLANES, LANES), F32)],
        compiler_params=_params("arbitrary", "arbitrary"),
        name="moe_router",
    )(x, mod, g, wr)

    meta = meta.reshape(n, LANES)
    ids = meta[:, 0:2].astype(jnp.int32)
    rank = meta[:, 2:4].astype(jnp.int32)
    counts = cnt[0, :n_exp].astype(jnp.int32)
    tiles = (counts + te - 1) // te
    tile_end = jnp.cumsum(tiles)
    row_start = (tile_end - tiles) * te
    start_of = jnp.sum(jnp.where(ids[:, :, None] == jnp.arange(n_exp, dtype=jnp.int32), row_start, 0), axis=-1)
    pos = (start_of + rank).reshape(-1)
    n_tiles = (2 * n) // te + n_exp
    r = jnp.arange(n_tiles, dtype=jnp.int32)
    tile_e = jnp.minimum(jnp.sum((r[:, None] >= tile_end[None, :]).astype(jnp.int32), axis=1), n_exp - 1)
    p = jnp.arange(n_tiles * te, dtype=jnp.int32)
    spare = 2 * n + ((p // te) % 2) * te + p % te
    a = jnp.arange(2 * n, dtype=jnp.int32)
    dst = spare.at[pos].set((a % 2) * n + a // 2, unique_indices=True)
    src = jnp.where(dst < 2 * n, dst % n, 0).reshape(n_tiles, 1, te)
    dst = jnp.concatenate([dst, 2 * n + te + jnp.arange(te, dtype=jnp.int32)]).reshape(n_tiles + 1, 1, te)

    smem = lambda fn: pl.BlockSpec((1, 1, te), fn, memory_space=pltpu.SMEM)
    y = pl.pallas_call(
        functools.partial(_expert_kernel, te, s, nf, n),
        grid_spec=pltpu.PrefetchScalarGridSpec(
            num_scalar_prefetch=1,
            grid=(n_tiles, nf),
            in_specs=[smem(lambda r, f, e: (0, 0, 0)),
                      smem(lambda r, f, e: (jnp.minimum(r + 1, n_tiles - 1), 0, 0)),
                      smem(lambda r, f, e: (jnp.where(r > 0, r - 1, n_tiles), 0, 0)),
                      smem(lambda r, f, e: (n_tiles - 1, 0, 0)),
                      pl.BlockSpec(memory_space=pl.ANY),
                      pl.BlockSpec((1, 1, d, tf), lambda r, f, e: (layer, e[r], 0, f)),
                      pl.BlockSpec((1, 1, d, tf), lambda r, f, e: (layer, e[r], 0, f)),
                      pl.BlockSpec((1, 1, tf, d), lambda r, f, e: (layer, e[r], f, 0))],
            out_specs=pl.BlockSpec(memory_space=pl.ANY),
            scratch_shapes=[pltpu.VMEM((2, te * s, LANES), F32), pltpu.VMEM((2, te * s, LANES), F32),
                            pltpu.VMEM((te, d), BF16), pltpu.VMEM((te, d), F32),
                            pltpu.SemaphoreType.DMA((2,)), pltpu.SemaphoreType.DMA((2,))]),
        out_shape=jax.ShapeDtypeStruct(((2 * n + 2 * te) * s, LANES), F32),
        compiler_params=_params("arbitrary", "arbitrary"),
        name="moe_experts",
    )(tile_e, src, src, dst, dst, hx, wg, wu, wd)

    flat = pl.BlockSpec((tm, d), lambda i: (i, 0))
    out = pl.pallas_call(
        _combine_kernel,
        grid=(n // tm,),
        in_specs=[flat, pl.BlockSpec((tm, LANES), lambda i: (i, 0)),
                  pl.BlockSpec((1, MOD_ROWS, d), lambda i: (i // per_b, 0, 0)),
                  pl.BlockSpec((tm * s, LANES), lambda i: (i, 0)),
                  pl.BlockSpec((tm * s, LANES), lambda i: (i + n // tm, 0))],
        out_specs=flat,
        out_shape=jax.ShapeDtypeStruct((n, d), F32),
        compiler_params=_params("parallel"),
        name="moe_combine",
    )(x.reshape(n, d), meta, mod, y, y)
    return out.reshape(b, l, d)


def _attn_kernel(q_ref, kp_ref, kc_ref, kn_ref, vp_ref, vc_ref, vn_ref, kx_ref, vx_ref, tab_ref, vm_ref, o_ref):
    hd = LANES // 2
    n_pairs, tq = q_ref.shape[1], q_ref.shape[2]
    lane = lax.broadcasted_iota(jnp.int32, (tq, LANES), 1)
    n_key_tiles = 3 * tq // LANES
    nt = (((1,), (1,)), ((), ()))

    def pair(p, carry):
        q2 = q_ref[0, p]
        k2 = jnp.concatenate([kp_ref[0, p], kc_ref[0, p], kn_ref[0, p]], axis=0)
        v2 = jnp.concatenate([vp_ref[0, p], vc_ref[0, p], vn_ref[0, p]], axis=0)
        kx2 = kx_ref[0, p]
        vx2 = vx_ref[0, p]
        halves = []
        for hh in range(2):
            qm = jnp.where((lane >= hh * hd) & (lane < (hh + 1) * hd), q2, jnp.zeros_like(q2))
            s_loc = lax.dot_general(qm, k2, nt, preferred_element_type=F32)
            s_ctx = lax.dot_general(qm, kx2, nt, preferred_element_type=F32)
            h = 2 * p + hh
            bias = jnp.concatenate(
                [jnp.concatenate([tab_ref[h, WIN_R // 2 - 1 + 2 * j - qr]
                                  + vm_ref[0, qr * n_key_tiles + j:qr * n_key_tiles + j + 1, :]
                                  for j in range(n_key_tiles)], axis=1)
                 for qr in range(Q_ROWS)], axis=0)
            s_loc = s_loc + bias
            m = jnp.maximum(jnp.max(s_loc, axis=-1, keepdims=True), jnp.max(s_ctx, axis=-1, keepdims=True))
            p_loc = jnp.exp(s_loc - m)
            p_ctx = jnp.exp(s_ctx - m)
            den = jnp.sum(p_loc, axis=-1, keepdims=True) + jnp.sum(p_ctx, axis=-1, keepdims=True)
            halves.append((_dot(p_loc.astype(BF16), v2) + _dot(p_ctx.astype(BF16), vx2)) / den)
        o_ref[0, p] = jnp.where(lane < hd, halves[0], halves[1]).astype(BF16)
        return carry

    lax.fori_loop(0, n_pairs, pair, 0)


def _attn_tables(rpb, rows):
    w = GRID_W
    qc = np.arange(w)
    cs = np.clip(qc - WIN_C // 2, 0, w - WIN_C)
    kc = np.arange(w)
    in_win = (kc[None, :] >= cs[:, None]) & (kc[None, :] < cs[:, None] + WIN_C)
    rel_c = np.clip(kc[None, :] - qc[:, None] + WIN_C - 1, 0, 2 * WIN_C - 2)
    n_m = 2 * WIN_R - 2
    rel_r = np.arange(n_m)[:, None] + np.arange(2)[None, :]
    pick = jnp.asarray(np.arange(2 * WIN_C - 1)[:, None, None] == rel_c[None], F32)
    g = jnp.einsum("hmdc,cqk->hmdqk", rpb[:, rel_r], pick, precision=lax.Precision.HIGHEST)
    g = jnp.where(in_win[None, None, None], g, NEG)
    tab = jnp.transpose(g, (0, 1, 3, 2, 4)).reshape(rpb.shape[0], n_m, w, 2 * w).astype(F32)

    key_tiles = 3 * Q_ROWS // 2
    nblk = rows // Q_ROWS
    vm = np.full((3, Q_ROWS * key_tiles, 2 * w), NEG, np.float32)
    for case, blk in enumerate((0, 1, nblk - 1)):
        r0 = blk * Q_ROWS
        for qr in range(Q_ROWS):
            rs = min(max(r0 + qr - WIN_R // 2, 0), rows - WIN_R)
            for j in range(key_tiles):
                for dr in range(2):
                    krow = r0 - Q_ROWS + 2 * j + dr
                    if rs <= krow < rs + WIN_R:
                        vm[case, qr * key_tiles + j, dr * w:(dr + 1) * w] = 0.0
    return tab, jnp.asarray(vm)


def _attention(q, k, v, kx, vx, rpb):
    b, pairs, l, _ = q.shape
    rows = l // GRID_W
    assert GRID_W * 2 == LANES and Q_ROWS == WIN_R // 2 and rows % Q_ROWS == 0 and rows // Q_ROWS >= 3
    tq = Q_ROWS * GRID_W
    nblk = rows // Q_ROWS
    ctx = kx.shape[2]
    tab, vm = _attn_tables(rpb, rows)
    blk = lambda fn: pl.BlockSpec((1, pairs, tq, LANES), fn)
    prev = blk(lambda bi, i: (bi, 0, jnp.maximum(i - 1, 0), 0))
    cur = blk(lambda bi, i: (bi, 0, i, 0))
    nxt = blk(lambda bi, i: (bi, 0, jnp.minimum(i + 1, nblk - 1), 0))
    cx = pl.BlockSpec((1, pairs, ctx, LANES), lambda bi, i: (bi, 0, 0, 0))
    case = lambda bi, i: (jnp.where(i > 0, 1, 0) + jnp.where(i == nblk - 1, 1, 0), 0, 0)
    return pl.pallas_call(
        _attn_kernel,
        grid=(b, nblk),
        in_specs=[cur, prev, cur, nxt, prev, cur, nxt, cx, cx,
                  pl.BlockSpec(tab.shape, lambda bi, i: (0, 0, 0, 0)),
                  pl.BlockSpec((1,) + vm.shape[1:], case)],
        out_specs=cur,
        out_shape=jax.ShapeDtypeStruct((b, pairs, l, LANES), BF16),
        compiler_params=_params("parallel", "parallel"),
        name="attention",
    )(q, k, k, k, v, v, v, kx, vx, tab, vm)


def _short_conv(x, mod, g, w_in, conv_w, w_out):
    b_gate, u = _mixer_in("sc", x, mod, g, w_in.astype(BF16), (), 2)
    return _mixer_out(_sc_out_kernel, "short_conv_out", x, mod, (b_gate, u), u, (conv_w, w_out.astype(BF16)))


def _conformer(x, mod, g, w1, b1, dw, dw_b, ln_g, ln_b, w2, b2):
    assert dw.shape[0] // 2 <= HALO
    row = lambda a: a.reshape(1, -1)
    (u,) = _mixer_in("cf", x, mod, g, w1.astype(BF16), (row(b1),), 1)
    body = functools.partial(_cf_out_kernel, dw.shape[0])
    return _mixer_out(body, "conformer_out", x, mod, (u,), u,
                      (dw, row(dw_b), row(ln_g), row(ln_b), w2.astype(BF16), row(b2)), tm_pref=256)


def _qkv(x, mod, g, w_qkv, q_g, k_g):
    d = x.shape[-1]
    hd = q_g.shape[0]
    heads = d // hd
    w = min(MXU_DIM, d)
    bd = jnp.asarray(np.kron(np.eye(w // hd), np.ones((hd, hd))), BF16)
    qg = (jnp.tile(q_g, heads) * hd ** -0.5).reshape(1, d)
    kg = jnp.tile(k_g, heads).reshape(1, d)
    return _mixer_in("qkv", x, mod, g, w_qkv.astype(BF16), (qg, kg, bd), 3, hd=hd)


def kernel(x, c, ctx, c_ctx, ada_w, ada_b, norm_mix_g, norm_ffn_g, sc_w_in, sc_conv_w, sc_w_out, cf_w1, cf_b1, cf_dw, cf_dw_b, cf_ln_g, cf_ln_b, cf_w2, cf_b2, na_w_qkv, na_q_g, na_k_g, na_rpb, na_w_out, ffn_w_gate, ffn_w_up, ffn_w_down, moe_router, moe_w_gate, moe_w_up, moe_w_down):
    b, l, d = x.shape
    depth = ada_w.shape[0]
    rb = -(-(b + 1) // SUBLANES) * SUBLANES
    cvec = jnp.zeros((rb, d), F32).at[:b].set(c).at[b].set(c_ctx)
    ada = _ada(cvec, ada_w, ada_b)
    pad = ((0, 0), (0, MOD_ROWS - N_MOD), (0, 0))
    moe_wg, moe_wu, moe_wd = moe_w_gate.astype(BF16), moe_w_up.astype(BF16), moe_w_down.astype(BF16)

    readers = [i for i in range(depth) if i % N_MIXERS == 2]
    last_reader = readers[-1] if readers else -1
    for i in range(depth):
        m, j, f = i % N_MIXERS, i // N_MIXERS, i // 2
        upd = i < last_reader
        mx = jnp.pad(ada[i, :b].reshape(b, N_MOD, d), pad)
        mc = jnp.broadcast_to(jnp.pad(ada[i, b].reshape(1, N_MOD, d), pad), (b, MOD_ROWS, d))
        gm = norm_mix_g[i].reshape(1, d)
        gf = norm_ffn_g[i].reshape(1, d)
        streams = [(x, mx)] + ([(ctx, mc)] if upd else [])
        if m == 0:
            mixed = [_short_conv(s, md, gm, sc_w_in[j], sc_conv_w[j], sc_w_out[j]) for s, md in streams]
        elif m == 1:
            mixed = [_conformer(s, md, gm, cf_w1[j], cf_b1[j], cf_dw[j], cf_dw_b[j], cf_ln_g[j], cf_ln_b[j],
                                cf_w2[j], cf_b2[j]) for s, md in streams]
        else:
            assert not upd, "a context update after an attention layer is not needed for this depth"
            q, k, v = _qkv(x, mx, gm, na_w_qkv[j], na_q_g[j], na_k_g[j])
            _, kc, vc = _qkv(ctx, mc, gm, na_w_qkv[j], na_q_g[j], na_k_g[j])
            mixed = [_attn_out(x, mx, _attention(q, k, v, kc, vc, na_rpb[j]), na_w_out[j].astype(BF16))]
        outs = []
        for s, md in zip(mixed, (mx, mc)):
            if i % 2 == 0:
                outs.append(_ffn(s, md, gf, ffn_w_gate[f].astype(BF16), ffn_w_up[f].astype(BF16),
                                 ffn_w_down[f].astype(BF16)))
            else:
                outs.append(_moe(s, md, gf, moe_router[f], moe_wg, moe_wu, moe_wd, f))
        x = outs[0]
        if upd:
            ctx = outs[1]
    return x
```

```python
import functools

import jax
import jax.numpy as jnp
import numpy as np
from jax import lax
from jax.experimental import pallas as pl
from jax.experimental.pallas import tpu as pltpu

GRID_W = 64
WIN_R = 8
WIN_C = 16
N_MIXERS = 3
N_MOD = 6
EPS = 1e-6

LANES = 128
SUBLANES = 8
MXU_DIM = 256
VMEM_LIMIT_BYTES = 56 * 1024 * 1024

HALO = 16
Q_ROWS = 4
NEG = -1e30
LOG2E = 1.4426950408889634
MOD_ROWS = 8

F32 = jnp.float32
BF16 = jnp.bfloat16


def _params(*sem):
    return pltpu.CompilerParams(dimension_semantics=sem, vmem_limit_bytes=VMEM_LIMIT_BYTES)


def _tile(n, pref):
    t = min(n, pref)
    while n % t or (t % HALO and t != n):
        t -= 1
    return t


def _dot(a, b):
    return jnp.dot(a, b, preferred_element_type=F32)


def _norm_mod(x, g, shift, scale):
    ms = jnp.mean(x * x, axis=-1, keepdims=True)
    return (x * lax.rsqrt(ms + EPS) * g) * (1.0 + scale) + shift


def _silu(x):
    return x * jax.nn.sigmoid(x)


def _lane_chunks(x):
    return [x[:, c * LANES:(c + 1) * LANES] for c in range(x.shape[1] // LANES)]


def _ada_kernel(c_ref, w_ref, b_ref, o_ref):
    o_ref[0] = jnp.dot(_silu(c_ref[...]), w_ref[0], precision=lax.Precision.HIGHEST,
                       preferred_element_type=F32) + b_ref[0]


def _ada(cvec, ada_w, ada_b):
    depth, d, nd = ada_w.shape
    rb = cvec.shape[0]
    tn = _tile(nd, 1024)
    return pl.pallas_call(
        _ada_kernel,
        grid=(depth, nd // tn),
        in_specs=[pl.BlockSpec((rb, d), lambda i, j: (0, 0)),
                  pl.BlockSpec((1, d, tn), lambda i, j: (i, 0, j)),
                  pl.BlockSpec((1, 1, tn), lambda i, j: (i, 0, j))],
        out_specs=pl.BlockSpec((1, rb, tn), lambda i, j: (i, 0, j)),
        out_shape=jax.ShapeDtypeStruct((depth, rb, nd), F32),
        compiler_params=_params("arbitrary", "arbitrary"),
        name="ada",
    )(cvec, ada_w, ada_b.reshape(depth, 1, nd))


def _head_norm(t, gain, bd_ref, hd):
    tt = t * t
    hi = tt.astype(BF16)
    lo = (tt - hi.astype(F32)).astype(BF16)
    bd = bd_ref[...]
    w = bd.shape[0]
    parts = []
    for c in range(t.shape[1] // w):
        sl = slice(c * w, (c + 1) * w)
        parts.append(_dot(hi[:, sl], bd) + _dot(lo[:, sl], bd))
    ss = jnp.concatenate(parts, axis=1) if len(parts) > 1 else parts[0]
    return t * lax.rsqrt(ss * (1.0 / hd) + EPS) * gain


def _in_kernel(mode, tn, hd, x_ref, mod_ref, g_ref, w_ref, *rest):
    mod = mod_ref[0]
    hx = _norm_mod(x_ref[0], g_ref[...], mod[0:1], mod[1:2]).astype(BF16)
    d = hx.shape[1]

    def mm(part, j):
        lo = part * d + j * tn
        return _dot(hx, w_ref[:, lo:lo + tn])

    for j in range(d // tn):
        sl = slice(j * tn, (j + 1) * tn)
        if mode == "sc":
            b_out, u_out = rest
            b_out[0, :, sl] = mm(0, j).astype(BF16)
            u_out[0, :, sl] = (mm(1, j) * mm(2, j)).astype(BF16)
        elif mode == "cf":
            bias_ref, u_out = rest
            a = mm(0, j) + bias_ref[:, sl]
            gte = mm(1, j) + bias_ref[:, d + j * tn:d + (j + 1) * tn]
            u_out[0, :, sl] = (a * jax.nn.sigmoid(gte)).astype(BF16)
        else:
            qg_ref, kg_ref, bd_ref, q_out, k_out, v_out = rest
            q = _head_norm(mm(0, j), qg_ref[:, sl], bd_ref, hd).astype(BF16)
            k = _head_norm(mm(1, j), kg_ref[:, sl], bd_ref, hd).astype(BF16)
            v = mm(2, j).astype(BF16)
            for c in range(tn // LANES):
                pair = j * (tn // LANES) + c
                q_out[0, pair] = q[:, c * LANES:(c + 1) * LANES]
                k_out[0, pair] = k[:, c * LANES:(c + 1) * LANES]
                v_out[0, pair] = v[:, c * LANES:(c + 1) * LANES]


def _mixer_in(mode, x, mod, g, w, extras, n_out, hd=1, tm_pref=512):
    b, l, d = x.shape
    tm = _tile(l, tm_pref)
    tn = _tile(d, 512)
    full = lambda a: pl.BlockSpec(a.shape, lambda bi, i: (0,) * a.ndim)
    tok = pl.BlockSpec((1, tm, d), lambda bi, i: (bi, i, 0))
    if mode == "qkv":
        out_spec = pl.BlockSpec((1, d // LANES, tm, LANES), lambda bi, i: (bi, 0, i, 0))
        out_shape = jax.ShapeDtypeStruct((b, d // LANES, l, LANES), BF16)
    else:
        out_spec, out_shape = tok, jax.ShapeDtypeStruct((b, l, d), BF16)
    return pl.pallas_call(
        functools.partial(_in_kernel, mode, tn, hd),
        grid=(b, l // tm),
        in_specs=[tok, pl.BlockSpec((1, MOD_ROWS, d), lambda bi, i: (bi, 0, 0)), full(g), full(w)]
                 + [full(e) for e in extras],
        out_specs=[out_spec] * n_out,
        out_shape=[out_shape] * n_out,
        compiler_params=_params("parallel", "parallel"),
        name="mixer_in_" + mode,
    )(x, mod, g, w, *extras)


def _with_halo(u_ref, up_ref, un_ref):
    i = pl.program_id(1)
    has_prev = jnp.where(i > 0, 1.0, 0.0)
    has_next = jnp.where(i < pl.num_programs(1) - 1, 1.0, 0.0)
    return jnp.concatenate([up_ref[0].astype(F32) * has_prev, u_ref[0].astype(F32),
                            un_ref[0].astype(F32) * has_next], axis=0)


def _sc_out_kernel(x_ref, mod_ref, b_ref, u_ref, up_ref, un_ref, cw_ref, w_ref, o_ref):
    ext = _with_halo(u_ref, up_ref, un_ref)
    n = ext.shape[0]
    tm = n - 2 * HALO
    taps = cw_ref.shape[0]
    conv = None
    for k in range(taps):
        off = k - taps // 2
        sh = ext if off == 0 else pltpu.roll(ext, (-off) % n, axis=0)
        term = cw_ref[k:k + 1, :] * sh[HALO:HALO + tm]
        conv = term if conv is None else conv + term
    z = (b_ref[0].astype(F32) * conv).astype(BF16)
    o_ref[0] = x_ref[0] + mod_ref[0][2:3] * _dot(z, w_ref[...])


def _cf_out_kernel(taps, x_ref, mod_ref, u_ref, up_ref, un_ref, dw_ref, dwb_ref, lng_ref, lnb_ref,
                   w_ref, b2_ref, o_ref):
    ext = _with_halo(u_ref, up_ref, un_ref)
    n = ext.shape[0]
    tm = n - 2 * HALO
    base = HALO - taps // 2
    acc = None
    for s in range(SUBLANES):
        sh = ext if s == 0 else pltpu.roll(ext, n - s, axis=0)
        for p in range((base + taps - 1) // SUBLANES + 1):
            k = SUBLANES * p + s - base
            if 0 <= k < taps:
                term = dw_ref[k:k + 1, :] * sh[SUBLANES * p:SUBLANES * p + tm]
                acc = term if acc is None else acc + term
    u2 = acc + dwb_ref[...]
    mu = jnp.mean(u2, axis=-1, keepdims=True)
    cen = u2 - mu
    var = jnp.mean(cen * cen, axis=-1, keepdims=True)
    y = cen * lax.rsqrt(var + EPS) * lng_ref[...] + lnb_ref[...]
    y = _dot(_silu(y).astype(BF16), w_ref[...]) + b2_ref[...]
    o_ref[0] = x_ref[0] + mod_ref[0][2:3] * y


def _mixer_out(body, name, x, mod, toks, halo_of, consts, tm_pref=512):
    b, l, d = x.shape
    tm = _tile(l, tm_pref)
    per = tm // HALO
    last = l // HALO - 1
    tok = pl.BlockSpec((1, tm, d), lambda bi, i: (bi, i, 0))
    full = lambda a: pl.BlockSpec(a.shape, lambda bi, i: (0,) * a.ndim)
    in_specs = ([tok, pl.BlockSpec((1, MOD_ROWS, d), lambda bi, i: (bi, 0, 0))] + [tok] * len(toks)
                + [pl.BlockSpec((1, HALO, d), lambda bi, i: (bi, jnp.maximum(i * per - 1, 0), 0)),
                   pl.BlockSpec((1, HALO, d), lambda bi, i: (bi, jnp.minimum((i + 1) * per, last), 0))]
                + [full(c) for c in consts])
    return pl.pallas_call(
        body,
        grid=(b, l // tm),
        in_specs=in_specs,
        out_specs=tok,
        out_shape=jax.ShapeDtypeStruct((b, l, d), F32),
        compiler_params=_params("parallel", "parallel"),
        name=name,
    )(x, mod, *toks, halo_of, halo_of, *consts)


def _attn_out_kernel(x_ref, mod_ref, a_ref, w_ref, o_ref):
    a = jnp.concatenate([a_ref[0, p] for p in range(a_ref.shape[1])], axis=1)
    o_ref[0] = x_ref[0] + mod_ref[0][2:3] * _dot(a, w_ref[...])


def _attn_out(x, mod, a, w, tm_pref=512):
    b, l, d = x.shape
    tm = _tile(l, tm_pref)
    tok = pl.BlockSpec((1, tm, d), lambda bi, i: (bi, i, 0))
    return pl.pallas_call(
        _attn_out_kernel,
        grid=(b, l // tm),
        in_specs=[tok, pl.BlockSpec((1, MOD_ROWS, d), lambda bi, i: (bi, 0, 0)),
                  pl.BlockSpec((1, d // LANES, tm, LANES), lambda bi, i: (bi, 0, i, 0)),
                  pl.BlockSpec(w.shape, lambda bi, i: (0, 0))],
        out_specs=tok,
        out_shape=jax.ShapeDtypeStruct((b, l, d), F32),
        compiler_params=_params("parallel", "parallel"),
        name="attn_out",
    )(x, mod, a, w)


def _swiglu_step(hx, wg_ref, wu_ref, wd_ref, acc_s):
    tf = wg_ref.shape[1]
    sub = MXU_DIM if tf % MXU_DIM == 0 else tf
    acc = acc_s[...]
    for c in range(tf // sub):
        sl = slice(c * sub, (c + 1) * sub)
        g = _dot(hx, wg_ref[:, sl])
        u = _dot(hx, wu_ref[:, sl])
        acc = acc + _dot((_silu(g) * u).astype(BF16), wd_ref[sl, :])
    acc_s[...] = acc


def _ffn_kernel(x_ref, mod_ref, g_ref, wg_ref, wu_ref, wd_ref, o_ref, hx_s, acc_s):
    f = pl.program_id(2)

    @pl.when(f == 0)
    def _():
        mod = mod_ref[0]
        hx_s[...] = _norm_mod(x_ref[0], g_ref[...], mod[3:4], mod[4:5]).astype(BF16)
        acc_s[...] = jnp.zeros_like(acc_s)

    _swiglu_step(hx_s[...], wg_ref, wu_ref, wd_ref, acc_s)

    @pl.when(f == pl.num_programs(2) - 1)
    def _():
        o_ref[0] = x_ref[0] + mod_ref[0][5:6] * acc_s[...]


def _ffn_chunk(fdim, pref):
    best = LANES
    for k in range(1, fdim // LANES + 1):
        t = k * LANES
        if fdim % t == 0 and t <= pref:
            best = t
    return best


def _ffn(x, mod, g, wg, wu, wd, tm_pref=512, tf_pref=2816):
    b, l, d = x.shape
    fdim = wg.shape[1]
    tm = _tile(l, tm_pref)
    tf = _ffn_chunk(fdim, tf_pref)
    tok = pl.BlockSpec((1, tm, d), lambda bi, i, f: (bi, i, 0))
    return pl.pallas_call(
        _ffn_kernel,
        grid=(b, l // tm, fdim // tf),
        in_specs=[tok, pl.BlockSpec((1, MOD_ROWS, d), lambda bi, i, f: (bi, 0, 0)),
                  pl.BlockSpec(g.shape, lambda bi, i, f: (0, 0)),
                  pl.BlockSpec((d, tf), lambda bi, i, f: (0, f)),
                  pl.BlockSpec((d, tf), lambda bi, i, f: (0, f)),
                  pl.BlockSpec((tf, d), lambda bi, i, f: (f, 0))],
        out_specs=tok,
        out_shape=jax.ShapeDtypeStruct((b, l, d), F32),
        scratch_shapes=[pltpu.VMEM((tm, d), BF16), pltpu.VMEM((tm, d), F32)],
        compiler_params=_params("parallel", "parallel", "arbitrary"),
        name="ffn",
    )(x, mod, g, wg, wu, wd)


def _to_token_tiled(ref, x):
    s = x.shape[1] // LANES
    for c, chunk in enumerate(_lane_chunks(x)):
        ref[pl.ds(c, x.shape[0], stride=s), :] = chunk


def _from_token_tiled(ref, tokens):
    s = ref.shape[0] // tokens
    return jnp.concatenate([ref[pl.ds(c, tokens, stride=s), :] for c in range(s)], axis=1)


def _router_kernel(n_exp, x_ref, mod_ref, g_ref, wr_ref, hx_ref, meta_ref, cnt_ref, carry_s):
    @pl.when((pl.program_id(0) == 0) & (pl.program_id(1) == 0))
    def _():
        carry_s[...] = jnp.zeros_like(carry_s)

    mod = mod_ref[0]
    hx = _norm_mod(x_ref[0], g_ref[...], mod[3:4], mod[4:5])
    _to_token_tiled(hx_ref, hx)
    tm = hx.shape[0]
    hi = hx.astype(BF16)
    lo = (hx - hi.astype(F32)).astype(BF16)
    logits = _dot(hi, wr_ref[0]) + (_dot(hi, wr_ref[1]) + _dot(lo, wr_ref[0]))
    lane = lax.broadcasted_iota(jnp.int32, (tm, LANES), 1).astype(F32)
    logits = jnp.where(lane < n_exp, logits, -jnp.inf)
    m1 = jnp.max(logits, axis=-1, keepdims=True)
    i1 = jnp.min(jnp.where(logits == m1, lane, float(LANES)), axis=-1, keepdims=True)
    rest = jnp.where(lane == i1, -jnp.inf, logits)
    m2 = jnp.max(rest, axis=-1, keepdims=True)
    i2 = jnp.min(jnp.where(rest == m2, lane, float(LANES)), axis=-1, keepdims=True)
    e = jnp.exp(m2 - m1)
    p1 = 1.0 / (1.0 + e)
    p2 = e / (1.0 + e)
    oh1 = lane == i1
    oh2 = lane == i2
    a = jnp.where(oh1 | oh2, 1.0, 0.0)
    below = jnp.where(lax.broadcasted_iota(jnp.int32, (tm, tm), 1)
                      < lax.broadcasted_iota(jnp.int32, (tm, tm), 0), 1.0, 0.0).astype(BF16)
    pre = _dot(below, a.astype(BF16)) + carry_s[0:1, :]
    r1 = jnp.sum(jnp.where(oh1, pre, 0.0), axis=-1, keepdims=True)
    r2 = jnp.sum(jnp.where(oh2, pre, 0.0), axis=-1, keepdims=True)
    carry_s[...] = carry_s[...] + jnp.sum(a, axis=0, keepdims=True)
    cnt_ref[...] = carry_s[...]
    meta = jnp.zeros((tm, LANES), F32)
    for col, val in enumerate((i1, i2, r1, r2, p1, p2)):
        meta = jnp.where(lane == col, val, meta)
    meta_ref[0] = meta


def _expert_kernel(te, s, nf, n_tok, te_ref, src0_ref, src_ref, dst_ref, dstl_ref, hx_hbm, wg_ref, wu_ref, wd_ref,
                   y_hbm, xbuf, ybuf, hx_s, acc_s, gsem, ssem):
    del te_ref
    r = pl.program_id(0)
    f = pl.program_id(1)
    rows = te * s
    per = te // nf
    cur = r % 2
    nxt = 1 - cur
    spare0 = 2 * n_tok

    def fetch(i, tok, slot):
        return pltpu.make_async_copy(hx_hbm.at[pl.ds(pl.multiple_of(tok * s, s), s)],
                                     xbuf.at[slot, pl.ds(pl.multiple_of(i * s, s), s)], gsem.at[slot])

    def send(i, row, slot):
        return pltpu.make_async_copy(ybuf.at[slot, pl.ds(pl.multiple_of(i * s, s), s)],
                                     y_hbm.at[pl.ds(pl.multiple_of(row * s, s), s)], ssem.at[slot])

    def all_fetched(slot):
        return pltpu.make_async_copy(hx_hbm.at[pl.ds(0, rows)], xbuf.at[slot], gsem.at[slot])

    def all_sent(slot):
        return pltpu.make_async_copy(ybuf.at[slot], y_hbm.at[pl.ds(0, rows)], ssem.at[slot])

    @pl.when((r == 0) & (f == 0))
    def _():
        ybuf[...] = jnp.zeros_like(ybuf)
        clear = pltpu.make_async_copy(ybuf.at[0], y_hbm.at[pl.ds(spare0 * s, rows)], ssem.at[0])
        clear.start()
        clear.wait()

        def first(i, carry):
            fetch(i, src0_ref[0, 0, i], 0).start()
            return carry

        lax.fori_loop(0, te, first, 0)

    @pl.when(f == 0)
    def _():
        all_fetched(cur).wait()
        hx_s[...] = _from_token_tiled(xbuf.at[cur], te).astype(BF16)
        acc_s[...] = jnp.zeros_like(acc_s)

    for j in range(per):
        i = f * per + j
        fetch(i, src_ref[0, 0, i], nxt).start()
        send(i, dst_ref[0, 0, i], nxt).start()

    _swiglu_step(hx_s[...], wg_ref.at[0, 0], wu_ref.at[0, 0], wd_ref.at[0, 0], acc_s)

    @pl.when(f == nf - 1)
    def _():
        @pl.when(r > 0)
        def _():
            all_sent(cur).wait()

        _to_token_tiled(ybuf.at[cur], acc_s[...])

        @pl.when(r == pl.num_programs(0) - 1)
        def _():
            def final(i, carry):
                send(i, dstl_ref[0, 0, i], cur).start()
                return carry

            lax.fori_loop(0, te, final, 0)
            all_sent(nxt).wait()
            all_sent(cur).wait()
            all_fetched(nxt).wait()


def _combine_kernel(x_ref, meta_ref, mod_ref, y1_ref, y2_ref, o_ref):
    tc = x_ref.shape[0]
    meta = meta_ref[...]
    y = meta[:, 4:5] * _from_token_tiled(y1_ref, tc) + meta[:, 5:6] * _from_token_tiled(y2_ref, tc)
    o_ref[...] = x_ref[...] + mod_ref[0][5:6] * y


def _moe(x, mod, g, w_router, wg, wu, wd, layer, tm_pref=512, te_pref=512, tf_pref=1792):
    b, l, d = x.shape
    n = b * l
    s = d // LANES
    n_exp, _, fdim = wg.shape[1:]
    tm = _tile(l, tm_pref)
    te = _tile(2 * n, min(te_pref, max(tm_pref // 2, n // n_exp)))
    tf = _ffn_chunk(fdim, tf_pref)
    nf = fdim // tf
    assert te % nf == 0 and n % tm == 0 and (2 * te) % tm == 0
    wr = jnp.zeros((d, LANES), F32).at[:, :n_exp].set(w_router)
    wr_hi = wr.astype(BF16)
    wr = jnp.stack([wr_hi, (wr - wr_hi.astype(F32)).astype(BF16)])
    per_b = l // tm

    hx, meta, cnt = pl.pallas_call(
        functools.partial(_router_kernel, n_exp),
        grid=(b, per_b),
        in_specs=[pl.BlockSpec((1, tm, d), lambda bi, i: (bi, i, 0)),
                  pl.BlockSpec((1, MOD_ROWS, d), lambda bi, i: (bi, 0, 0)),
                  pl.BlockSpec(g.shape, lambda bi, i: (0, 0)),
                  pl.BlockSpec(wr.shape, lambda bi, i: (0, 0, 0))],
        out_specs=[pl.BlockSpec((tm * s, LANES), lambda bi, i: (bi * per_b + i, 0)),
                   pl.BlockSpec((1, tm, LANES), lambda bi, i: (bi, i, 0)),
                   pl.BlockSpec((SUBLANES, LANES), lambda bi, i: (0, 0))],
        out_shape=[jax.ShapeDtypeStruct((n * s, LANES), F32), jax.ShapeDtypeStruct((b, l, LANES), F32),
                   jax.ShapeDtypeStruct((SUBLANES, LANES), F32)],
        scratch_shapes=[pltpu.VMEM((SUBLANES, LANES), F32)],
        compiler_params=_params("arbitrary", "arbitrary"),
        name="moe_router",
    )(x, mod, g, wr)

    meta = meta.reshape(n, LANES)
    ids = meta[:, 0:2].astype(jnp.int32)
    rank = meta[:, 2:4].astype(jnp.int32)
    counts = cnt[0, :n_exp].astype(jnp.int32)
    tiles = (counts + te - 1) // te
    tile_end = jnp.cumsum(tiles)
    row_start = (tile_end - tiles) * te
    start_of = jnp.sum(jnp.where(ids[:, :, None] == jnp.arange(n_exp, dtype=jnp.int32), row_start, 0), axis=-1)
    pos = (start_of + rank).reshape(-1)
    n_tiles = (2 * n) // te + n_exp
    r = jnp.arange(n_tiles, dtype=jnp.int32)
    tile_e = jnp.minimum(jnp.sum((r[:, None] >= tile_end[None, :]).astype(jnp.int32), axis=1), n_exp - 1)
    p = jnp.arange(n_tiles * te, dtype=jnp.int32)
    spare = 2 * n + ((p // te) % 2) * te + p % te
    a = jnp.arange(2 * n, dtype=jnp.int32)
    dst = spare.at[pos].set((a % 2) * n + a // 2, unique_indices=True)
    src = jnp.where(dst < 2 * n, dst % n, 0).reshape(n_tiles, 1, te)
    dst = jnp.concatenate([dst, 2 * n + te + jnp.arange(te, dtype=jnp.int32)]).reshape(n_tiles + 1, 1, te)

    smem = lambda fn: pl.BlockSpec((1, 1, te), fn, memory_space=pltpu.SMEM)
    y = pl.pallas_call(
        functools.partial(_expert_kernel, te, s, nf, n),
        grid_spec=pltpu.PrefetchScalarGridSpec(
            num_scalar_prefetch=1,
            grid=(n_tiles, nf),
            in_specs=[smem(lambda r, f, e: (0, 0, 0)),
                      smem(lambda r, f, e: (jnp.minimum(r + 1, n_tiles - 1), 0, 0)),
                      smem(lambda r, f, e: (jnp.where(r > 0, r - 1, n_tiles), 0, 0)),
                      smem(lambda r, f, e: (n_tiles - 1, 0, 0)),
                      pl.BlockSpec(memory_space=pl.ANY),
                      pl.BlockSpec((1, 1, d, tf), lambda r, f, e: (layer, e[r], 0, f)),
                      pl.BlockSpec((1, 1, d, tf), lambda r, f, e: (layer, e[r], 0, f)),
                      pl.BlockSpec((1, 1, tf, d), lambda r, f, e: (layer, e[r], f, 0))],
            out_specs=pl.BlockSpec(memory_space=pl.ANY),
            scratch_shapes=[pltpu.VMEM((2, te * s, LANES), F32), pltpu.VMEM((2, te * s, LANES), F32),
                            pltpu.VMEM((te, d), BF16), pltpu.VMEM((te, d), F32),
                            pltpu.SemaphoreType.DMA((2,)), pltpu.SemaphoreType.DMA((2,))]),
        out_shape=jax.ShapeDtypeStruct(((2 * n + 2 * te) * s, LANES), F32),
        compiler_params=_params("arbitrary", "arbitrary"),
        name="moe_experts",
    )(tile_e, src, src, dst, dst, hx, wg, wu, wd)

    flat = pl.BlockSpec((tm, d), lambda i: (i, 0))
    out = pl.pallas_call(
        _combine_kernel,
        grid=(n // tm,),
        in_specs=[flat, pl.BlockSpec((tm, LANES), lambda i: (i, 0)),
                  pl.BlockSpec((1, MOD_ROWS, d), lambda i: (i // per_b, 0, 0)),
                  pl.BlockSpec((tm * s, LANES), lambda i: (i, 0)),
                  pl.BlockSpec((tm * s, LANES), lambda i: (i + n // tm, 0))],
        out_specs=flat,
        out_shape=jax.ShapeDtypeStruct((n, d), F32),
        compiler_params=_params("parallel"),
        name="moe_combine",
    )(x.reshape(n, d), meta, mod, y, y)
    return out.reshape(b, l, d)


def _attn_kernel(q_ref, kp_ref, kc_ref, kn_ref, vp_ref, vc_ref, vn_ref, kx_ref, vx_ref, tab_ref, vm_ref, o_ref):
    hd = LANES // 2
    n_pairs, tq = q_ref.shape[1], q_ref.shape[2]
    lane = lax.broadcasted_iota(jnp.int32, (tq, LANES), 1)
    n_key_tiles = 3 * tq // LANES
    nt = (((1,), (1,)), ((), ()))

    def pair(p, carry):
        q2 = q_ref[0, p]
        k2 = jnp.concatenate([kp_ref[0, p], kc_ref[0, p], kn_ref[0, p]], axis=0)
        v2 = jnp.concatenate([vp_ref[0, p], vc_ref[0, p], vn_ref[0, p]], axis=0)
        kx2 = kx_ref[0, p]
        vx2 = vx_ref[0, p]
        halves = []
        for hh in range(2):
            qm = jnp.where((lane >= hh * hd) & (lane < (hh + 1) * hd), q2, jnp.zeros_like(q2))
            s_loc = lax.dot_general(qm, k2, nt, preferred_element_type=F32)
            s_ctx = lax.dot_general(qm, kx2, nt, preferred_element_type=F32)
            h = 2 * p + hh
            bias = jnp.concatenate(
                [jnp.concatenate([tab_ref[h, WIN_R // 2 - 1 + 2 * j - qr]
                                  + vm_ref[0, qr * n_key_tiles + j:qr * n_key_tiles + j + 1, :]
                                  for j in range(n_key_tiles)], axis=1)
                 for qr in range(Q_ROWS)], axis=0)
            s_loc = s_loc + bias
            m = jnp.maximum(jnp.max(s_loc, axis=-1, keepdims=True), jnp.max(s_ctx, axis=-1, keepdims=True))
            p_loc = jnp.exp2(s_loc - m)
            p_ctx = jnp.exp2(s_ctx - m)
            den = jnp.sum(p_loc, axis=-1, keepdims=True) + jnp.sum(p_ctx, axis=-1, keepdims=True)
            halves.append((_dot(p_loc.astype(BF16), v2) + _dot(p_ctx.astype(BF16), vx2)) / den)
        o_ref[0, p] = jnp.where(lane < hd, halves[0], halves[1]).astype(BF16)
        return carry

    lax.fori_loop(0, n_pairs, pair, 0, unroll=2 if n_pairs % 2 == 0 else 1)


def _attn_tables(rpb, rows):
    w = GRID_W
    qc = np.arange(w)
    cs = np.clip(qc - WIN_C // 2, 0, w - WIN_C)
    kc = np.arange(w)
    in_win = (kc[None, :] >= cs[:, None]) & (kc[None, :] < cs[:, None] + WIN_C)
    rel_c = np.clip(kc[None, :] - qc[:, None] + WIN_C - 1, 0, 2 * WIN_C - 2)
    n_m = 2 * WIN_R - 2
    rel_r = np.arange(n_m)[:, None] + np.arange(2)[None, :]
    pick = jnp.asarray(np.arange(2 * WIN_C - 1)[:, None, None] == rel_c[None], F32)
    g = jnp.einsum("hmdc,cqk->hmdqk", rpb[:, rel_r], pick, precision=lax.Precision.HIGHEST)
    g = jnp.where(in_win[None, None, None], g * LOG2E, NEG)
    tab = jnp.transpose(g, (0, 1, 3, 2, 4)).reshape(rpb.shape[0], n_m, w, 2 * w).astype(F32)

    key_tiles = 3 * Q_ROWS // 2
    nblk = rows // Q_ROWS
    vm = np.full((3, Q_ROWS * key_tiles, 2 * w), NEG, np.float32)
    for case, blk in enumerate((0, 1, nblk - 1)):
        r0 = blk * Q_ROWS
        for qr in range(Q_ROWS):
            rs = min(max(r0 + qr - WIN_R // 2, 0), rows - WIN_R)
            for j in range(key_tiles):
                for dr in range(2):
                    krow = r0 - Q_ROWS + 2 * j + dr
                    if rs <= krow < rs + WIN_R:
                        vm[case, qr * key_tiles + j, dr * w:(dr + 1) * w] = 0.0
    return tab, jnp.asarray(vm)


def _attention(q, k, v, kx, vx, rpb):
    b, pairs, l, _ = q.shape
    rows = l // GRID_W
    assert GRID_W * 2 == LANES and Q_ROWS == WIN_R // 2 and rows % Q_ROWS == 0 and rows // Q_ROWS >= 3
    tq = Q_ROWS * GRID_W
    nblk = rows // Q_ROWS
    ctx = kx.shape[2]
    tab, vm = _attn_tables(rpb, rows)
    blk = lambda fn: pl.BlockSpec((1, pairs, tq, LANES), fn)
    prev = blk(lambda bi, i: (bi, 0, jnp.maximum(i - 1, 0), 0))
    cur = blk(lambda bi, i: (bi, 0, i, 0))
    nxt = blk(lambda bi, i: (bi, 0, jnp.minimum(i + 1, nblk - 1), 0))
    cx = pl.BlockSpec((1, pairs, ctx, LANES), lambda bi, i: (bi, 0, 0, 0))
    case = lambda bi, i: (jnp.where(i > 0, 1, 0) + jnp.where(i == nblk - 1, 1, 0), 0, 0)
    return pl.pallas_call(
        _attn_kernel,
        grid=(b, nblk),
        in_specs=[cur, prev, cur, nxt, prev, cur, nxt, cx, cx,
                  pl.BlockSpec(tab.shape, lambda bi, i: (0, 0, 0, 0)),
                  pl.BlockSpec((1,) + vm.shape[1:], case)],
        out_specs=cur,
        out_shape=jax.ShapeDtypeStruct((b, pairs, l, LANES), BF16),
        compiler_params=_params("parallel", "parallel"),
        name="attention",
    )(q, k, k, k, v, v, v, kx, vx, tab, vm)


def _short_conv(x, mod, g, w_in, conv_w, w_out):
    b_gate, u = _mixer_in("sc", x, mod, g, w_in.astype(BF16), (), 2)
    return _mixer_out(_sc_out_kernel, "short_conv_out", x, mod, (b_gate, u), u, (conv_w, w_out.astype(BF16)))


def _conformer(x, mod, g, w1, b1, dw, dw_b, ln_g, ln_b, w2, b2):
    assert dw.shape[0] // 2 <= HALO
    row = lambda a: a.reshape(1, -1)
    (u,) = _mixer_in("cf", x, mod, g, w1.astype(BF16), (row(b1),), 1)
    body = functools.partial(_cf_out_kernel, dw.shape[0])
    return _mixer_out(body, "conformer_out", x, mod, (u,), u,
                      (dw, row(dw_b), row(ln_g), row(ln_b), w2.astype(BF16), row(b2)), tm_pref=256)


def _qkv(x, mod, g, w_qkv, q_g, k_g):
    d = x.shape[-1]
    hd = q_g.shape[0]
    heads = d // hd
    w = min(MXU_DIM, d)
    bd = jnp.asarray(np.kron(np.eye(w // hd), np.ones((hd, hd))), BF16)
    qg = (jnp.tile(q_g, heads) * (hd ** -0.5 * LOG2E)).reshape(1, d)
    kg = jnp.tile(k_g, heads).reshape(1, d)
    return _mixer_in("qkv", x, mod, g, w_qkv.astype(BF16), (qg, kg, bd), 3, hd=hd)


def kernel(x, c, ctx, c_ctx, ada_w, ada_b, norm_mix_g, norm_ffn_g, sc_w_in, sc_conv_w, sc_w_out, cf_w1, cf_b1, cf_dw, cf_dw_b, cf_ln_g, cf_ln_b, cf_w2, cf_b2, na_w_qkv, na_q_g, na_k_g, na_rpb, na_w_out, ffn_w_gate, ffn_w_up, ffn_w_down, moe_router, moe_w_gate, moe_w_up, moe_w_down):
    b, l, d = x.shape
    depth = ada_w.shape[0]
    rb = -(-(b + 1) // SUBLANES) * SUBLANES
    cvec = jnp.zeros((rb, d), F32).at[:b].set(c).at[b].set(c_ctx)
    ada = _ada(cvec, ada_w, ada_b)
    pad = ((0, 0), (0, MOD_ROWS - N_MOD), (0, 0))
    moe_wg, moe_wu, moe_wd = moe_w_gate.astype(BF16), moe_w_up.astype(BF16), moe_w_down.astype(BF16)

    readers = [i for i in range(depth) if i % N_MIXERS == 2]
    last_reader = readers[-1] if readers else -1
    for i in range(depth):
        m, j, f = i % N_MIXERS, i // N_MIXERS, i // 2
        upd = i < last_reader
        mx = jnp.pad(ada[i, :b].reshape(b, N_MOD, d), pad)
        mc = jnp.broadcast_to(jnp.pad(ada[i, b].reshape(1, N_MOD, d), pad), (b, MOD_ROWS, d))
        gm = norm_mix_g[i].reshape(1, d)
        gf = norm_ffn_g[i].reshape(1, d)
        streams = [(x, mx)] + ([(ctx, mc)] if upd else [])
        if m == 0:
            mixed = [_short_conv(s, md, gm, sc_w_in[j], sc_conv_w[j], sc_w_out[j]) for s, md in streams]
        elif m == 1:
            mixed = [_conformer(s, md, gm, cf_w1[j], cf_b1[j], cf_dw[j], cf_dw_b[j], cf_ln_g[j], cf_ln_b[j],
                                cf_w2[j], cf_b2[j]) for s, md in streams]
        else:
            assert not upd, "a context update after an attention layer is not needed for this depth"
            q, k, v = _qkv(x, mx, gm, na_w_qkv[j], na_q_g[j], na_k_g[j])
            _, kc, vc = _qkv(ctx, mc, gm, na_w_qkv[j], na_q_g[j], na_k_g[j])
            mixed = [_attn_out(x, mx, _attention(q, k, v, kc, vc, na_rpb[j]), na_w_out[j].astype(BF16))]
        outs = []
        for s, md in zip(mixed, (mx, mc)):
            if i % 2 == 0:
                outs.append(_ffn(s, md, gf, ffn_w_gate[f].astype(BF16), ffn_w_up[f].astype(BF16),
                                 ffn_w_down[f].astype(BF16)))
            else:
                outs.append(_moe(s, md, gf, moe_router[f], moe_wg, moe_wu, moe_wd, f))
        x = outs[0]
        if upd:
            ctx = outs[1]
    return x
```
